```python
import jax, jax.numpy as jnp
from jax import lax
import numpy as np

D_MODEL = 1024
BATCH = 16
SEQ = 2048
DEPTH = 1

N_META = 16
D_CONV_GRP = D_MODEL
CONV_WIDTH = 3
CONV_HEADS = 8
SSD_HEADS = 16
SSD_HEAD_DIM = 64
D_SSD = SSD_HEADS * SSD_HEAD_DIM
SSD_GROUPS = 4
D_STATE = 128
SSD_CONV = 4
CHUNK = 128
D_MIX = D_CONV_GRP + D_SSD
SSD_XBC = D_SSD + 2 * SSD_GROUPS * D_STATE
D_IN_PROJ = 3 * D_CONV_GRP + D_SSD + SSD_XBC + SSD_HEADS
N_EXPERTS = 32
TOP_K = 4
D_FF = D_MODEL
SWIGLU_ALPHA = 1.702
SWIGLU_LIMIT = 7.0
MOE_BLOCK = 128
EPS = 1e-5

kernel_name = "hymba_conv_ssd_moe_layer"


def rms_norm(x, w):
    xf = x.astype(jnp.float32)
    y = xf * lax.rsqrt(jnp.mean(xf * xf, axis=-1, keepdims=True) + EPS)
    return (y * w.astype(jnp.float32)).astype(x.dtype)


def grouped_rms_norm(x, w, groups):
    shp = x.shape
    xf = x.astype(jnp.float32).reshape(shp[:-1] + (groups, shp[-1] // groups))
    y = xf * lax.rsqrt(jnp.mean(xf * xf, axis=-1, keepdims=True) + EPS)
    return (y.reshape(shp) * w.astype(jnp.float32)).astype(x.dtype)


def causal_dwconv(u, w, b=None):
    k_w = w.shape[0]
    length = u.shape[1]
    up = jnp.pad(u, ((0, 0), (k_w - 1, 0), (0, 0)))
    y = up[:, 0:length] * w[0]
    for k in range(1, k_w):
        y = y + up[:, k:k + length] * w[k]
    if b is not None:
        y = y + b
    return y


def ssd_chunked(xdt, da, bm, cm):
    bsz, length = xdt.shape[0], xdt.shape[1]
    pad_front = (-length) % CHUNK

    def front_pad(t):
        return jnp.pad(t, [(0, 0), (pad_front, 0)] + [(0, 0)] * (t.ndim - 2))

    xdt, da, bm, cm = front_pad(xdt), front_pad(da), front_pad(bm), front_pad(cm)
    nc = (length + pad_front) // CHUNK
    r = SSD_HEADS // SSD_GROUPS
    xc = xdt.reshape(bsz, nc, CHUNK, SSD_GROUPS, r, SSD_HEAD_DIM)
    ac = da.reshape(bsz, nc, CHUNK, SSD_GROUPS, r)
    bc = bm.reshape(bsz, nc, CHUNK, SSD_GROUPS, D_STATE)
    cc = cm.reshape(bsz, nc, CHUNK, SSD_GROUPS, D_STATE)

    a_cs = jnp.cumsum(ac, axis=2)
    seg = a_cs[:, :, :, None] - a_cs[:, :, None, :]
    causal = jnp.tril(jnp.ones((CHUNK, CHUNK), dtype=bool))[:, :, None, None]
    decay = jnp.where(causal, jnp.exp(jnp.where(causal, seg, 0.0)), 0.0)

    cb = jnp.einsum('bclgn,bcsgn->bclsg', cc, bc)
    y_diag = jnp.einsum('bclsgr,bcsgrp->bclgrp', cb[..., None] * decay, xc)

    decay_to_end = jnp.exp(a_cs[:, :, -1:] - a_cs)
    states = jnp.einsum('bclgn,bclgrp->bcgrpn', bc, xc * decay_to_end[..., None])
    chunk_decay = jnp.exp(a_cs[:, :, -1])

    def step(h, inp):
        st, dec = inp
        return dec[..., None, None] * h + st, h

    h0 = jnp.zeros_like(states[:, 0])
    _, prev = lax.scan(step, h0, (jnp.moveaxis(states, 1, 0), jnp.moveaxis(chunk_decay, 1, 0)))
    prev = jnp.moveaxis(prev, 0, 1)
    y_off = jnp.einsum('bclgn,bcgrpn->bclgrp', cc, prev) * jnp.exp(a_cs)[..., None]

    y = (y_diag + y_off).reshape(bsz, nc * CHUNK, SSD_HEADS, SSD_HEAD_DIM)
    return y[:, pad_front:]


def hybrid_mixer(n, w_in, conv_short_w, conv_norm_w, ssd_conv_w, ssd_conv_b,
                 ssd_dt_bias, ssd_a_log, ssd_d, ssd_norm_w, w_out):
    bsz, length, _ = n.shape
    proj = jnp.einsum('bld,de->ble', n, w_in)
    cuts = list(np.cumsum([D_CONV_GRP, D_CONV_GRP, D_CONV_GRP, D_SSD, SSD_XBC]))
    c_b, c_c, c_x, z, xbc, dt_raw = jnp.split(proj, cuts, axis=-1)

    y_conv = c_b * causal_dwconv(c_c * c_x, conv_short_w)
    y_conv = grouped_rms_norm(y_conv, conv_norm_w, CONV_HEADS)

    xbc = jax.nn.silu(causal_dwconv(xbc, ssd_conv_w, ssd_conv_b))
    xs, bs, cs = jnp.split(xbc, [D_SSD, D_SSD + SSD_GROUPS * D_STATE], axis=-1)
    xs = xs.reshape(bsz, length, SSD_HEADS, SSD_HEAD_DIM).astype(jnp.float32)
    dt = jax.nn.softplus(dt_raw.astype(jnp.float32) + ssd_dt_bias.astype(jnp.float32))
    a = -jnp.exp(ssd_a_log.astype(jnp.float32))
    y_ssd = ssd_chunked(xs * dt[..., None], dt * a,
                        bs.astype(jnp.float32).reshape(bsz, length, SSD_GROUPS, D_STATE),
                        cs.astype(jnp.float32).reshape(bsz, length, SSD_GROUPS, D_STATE))
    y_ssd = y_ssd + ssd_d.astype(jnp.float32)[:, None] * xs
    y_ssd = y_ssd.reshape(bsz, length, D_SSD).astype(n.dtype) * jax.nn.silu(z)
    y_ssd = grouped_rms_norm(y_ssd, ssd_norm_w, SSD_GROUPS)

    y = jnp.concatenate([y_conv, y_ssd], axis=-1)
    return jnp.einsum('ble,ed->bld', y, w_out)


def moe_ffn(n, router_w, router_b, w_gate_up, b_gate_up, w_down, b_down):
    bsz, length, d = n.shape
    n_tok = bsz * length
    xt = n.reshape(n_tok, d)
    logits = (xt @ router_w + router_b).astype(jnp.float32)
    top_vals, top_idx = lax.top_k(logits, TOP_K)
    gates = jax.nn.softmax(top_vals, axis=-1).astype(n.dtype)

    n_assign = n_tok * TOP_K
    flat_e = top_idx.reshape(-1)
    flat_tok = jnp.arange(n_assign, dtype=jnp.int32) // TOP_K
    flat_g = gates.reshape(-1)
    order = jnp.argsort(flat_e)
    sorted_e = flat_e[order]
    counts = jnp.bincount(flat_e, length=N_EXPERTS)
    padded = ((counts + MOE_BLOCK - 1) // MOE_BLOCK) * MOE_BLOCK
    pad_end = jnp.cumsum(padded)
    pad_start = pad_end - padded
    grp_start = jnp.cumsum(counts) - counts
    dest = pad_start[sorted_e] + (jnp.arange(n_assign, dtype=jnp.int32) - grp_start[sorted_e])

    n_blocks = -(-(n_assign + N_EXPERTS * (MOE_BLOCK - 1)) // MOE_BLOCK)
    rows = n_blocks * MOE_BLOCK
    slot_tok = jnp.zeros((rows,), jnp.int32).at[dest].set(flat_tok[order])
    slot_gate = jnp.zeros((rows,), n.dtype).at[dest].set(flat_g[order])
    blk_expert = jnp.minimum(
        jnp.searchsorted(pad_end, jnp.arange(n_blocks, dtype=jnp.int32) * MOE_BLOCK, side='right'),
        N_EXPERTS - 1)

    def run_block(args):
        tok, g, e = args
        xb = xt[tok]
        gu = xb @ w_gate_up[e] + b_gate_up[e]
        glu = jnp.minimum(gu[:, :D_FF], SWIGLU_LIMIT)
        lin = jnp.clip(gu[:, D_FF:], -SWIGLU_LIMIT, SWIGLU_LIMIT)
        act = glu * jax.nn.sigmoid(SWIGLU_ALPHA * glu) * (lin + 1.0)
        return (act @ w_down[e] + b_down[e]) * g[:, None]

    yb = lax.map(run_block, (slot_tok.reshape(n_blocks, MOE_BLOCK),
                             slot_gate.reshape(n_blocks, MOE_BLOCK), blk_expert))
    out = jnp.zeros((n_tok, d), n.dtype).at[slot_tok].add(yb.reshape(rows, d))
    return out.reshape(bsz, length, d)


def setup_inputs(seed: int = 0) -> dict:
    key = jax.random.key(seed)
    ks = jax.random.split(key, 24)
    f32 = jnp.float32
    nrm = lambda k, shp, s: jax.random.normal(k, shp, f32) * s
    dt = jnp.exp(jax.random.uniform(ks[9], (DEPTH, SSD_HEADS), f32)
                 * (jnp.log(0.1) - jnp.log(0.001)) + jnp.log(0.001))
    return {
        "x": nrm(ks[0], (BATCH, SEQ, D_MODEL), 1.0),
        "meta_tokens": nrm(ks[1], (N_META, D_MODEL), 1.0),
        "norm_mix_w": 1.0 + nrm(ks[2], (DEPTH, D_MODEL), 0.02),
        "w_in": nrm(ks[3], (DEPTH, D_MODEL, D_IN_PROJ), D_MODEL ** -0.5),
        "conv_short_w": nrm(ks[4], (DEPTH, CONV_WIDTH, D_CONV_GRP), CONV_WIDTH ** -0.5),
        "conv_norm_w": 1.0 + nrm(ks[5], (DEPTH, D_CONV_GRP), 0.02),
        "ssd_conv_w": nrm(ks[6], (DEPTH, SSD_CONV, SSD_XBC), SSD_CONV ** -0.5),
        "ssd_conv_b": nrm(ks[7], (DEPTH, SSD_XBC), 0.01),
        "ssd_dt_bias": dt + jnp.log(-jnp.expm1(-dt)),
        "ssd_a_log": jnp.log(jax.random.uniform(ks[8], (DEPTH, SSD_HEADS), f32, 1.0, 16.0)),
        "ssd_d": 1.0 + nrm(ks[10], (DEPTH, SSD_HEADS), 0.1),
        "ssd_norm_w": 1.0 + nrm(ks[11], (DEPTH, D_SSD), 0.02),
        "w_out": nrm(ks[12], (DEPTH, D_MIX, D_MODEL), D_MIX ** -0.5),
        "norm_ffn_w": 1.0 + nrm(ks[13], (DEPTH, D_MODEL), 0.02),
        "router_w": nrm(ks[14], (DEPTH, D_MODEL, N_EXPERTS), D_MODEL ** -0.5),
        "router_b": nrm(ks[15], (DEPTH, N_EXPERTS), 0.01),
        "w_gate_up": nrm(ks[16], (DEPTH, N_EXPERTS, D_MODEL, 2 * D_FF), D_MODEL ** -0.5),
        "b_gate_up": nrm(ks[17], (DEPTH, N_EXPERTS, 2 * D_FF), 0.01),
        "w_down": nrm(ks[18], (DEPTH, N_EXPERTS, D_FF, D_MODEL), D_FF ** -0.5),
        "b_down": nrm(ks[19], (DEPTH, N_EXPERTS, D_MODEL), 0.01),
        "norm_final_w": 1.0 + nrm(ks[20], (D_MODEL,), 0.02),
    }


def reference(x, meta_tokens, norm_mix_w, w_in, conv_short_w, conv_norm_w, ssd_conv_w,
              ssd_conv_b, ssd_dt_bias, ssd_a_log, ssd_d, ssd_norm_w, w_out, norm_ffn_w,
              router_w, router_b, w_gate_up, b_gate_up, w_down, b_down, norm_final_w):
    bsz = x.shape[0]
    meta = jnp.broadcast_to(meta_tokens.astype(x.dtype)[None], (bsz, N_META, D_MODEL))
    h = jnp.concatenate([meta, x], axis=1)
    for l in range(DEPTH):
        h = h + hybrid_mixer(rms_norm(h, norm_mix_w[l]), w_in[l], conv_short_w[l], conv_norm_w[l],
                             ssd_conv_w[l], ssd_conv_b[l], ssd_dt_bias[l], ssd_a_log[l], ssd_d[l],
                             ssd_norm_w[l], w_out[l])
        h = h + moe_ffn(rms_norm(h, norm_ffn_w[l]), router_w[l], router_b[l], w_gate_up[l],
                        b_gate_up[l], w_down[l], b_down[l])
    h = rms_norm(h, norm_final_w)
    return h[:, N_META:]
```

```python
import functools

import jax
import jax.numpy as jnp
from jax import lax
from jax.experimental import pallas as pl
from jax.experimental.pallas import tpu as pltpu

F32 = jnp.float32
BF16 = jnp.bfloat16
I32 = jnp.int32

D_MODEL = 1024
N_META = 16
CONV_WIDTH = 3
CONV_HEADS = 8
SSD_HEADS = 16
SSD_HEAD_DIM = 64
D_SSD = SSD_HEADS * SSD_HEAD_DIM
SSD_GROUPS = 4
D_STATE = 128
SSD_CONV = 4
CHUNK = 128
N_EXPERTS = 32
TOP_K = 4
D_FF = 1024
SWIGLU_ALPHA = 1.702
SWIGLU_LIMIT = 7.0
EPS = 1e-5

LANES = 128
SUBLANES = 8
VMEM_LIMIT = 56 * 1024 * 1024

OFF_CB, OFF_CC, OFF_CX, OFF_Z, OFF_XBC, OFF_DT = 0, 1024, 2048, 3072, 4096, 6144
D_XBC = D_SSD + 2 * SSD_GROUPS * D_STATE
D_IN_PAD = OFF_DT + LANES
LANES_PER_ROW = D_MODEL // LANES

MIX_ROWS = 256
ROUTE_ROWS = 256
MOE_BLOCK = 256
DISPATCH_ROWS = 256
DMA_SLOTS = 64


def _rms(x, w):
    return x * lax.rsqrt(jnp.mean(x * x, axis=-1, keepdims=True) + EPS) * w


def _silu(x):
    return x * (1.0 / (1.0 + jnp.exp(-x)))


def _dot(a, b):
    return jnp.dot(a, b, preferred_element_type=F32)


def _dot_exact(a, b):
    return jnp.dot(a, b, preferred_element_type=F32, precision=lax.Precision.HIGHEST)


def _mixer_kernel(x_ref, st0_ref, tu0_ref, tx0_ref, nw_ref, win_ref, cw_ref, cnw_ref, xw_ref, xb_ref,
                  dtb_ref, alog_ref, dvec_ref, snw_ref, wout_ref,
                  h_ref, st_ref, tu_ref, tx_ref,
                  state, ubuf, xbuf, xc, zs, ymix, *, rows, n_inert):
    i = pl.program_id(1)
    n_chunks = rows // CHUNK

    @pl.when(i == 0)
    def _():
        state[...] = st0_ref[...]
        ubuf[0:SUBLANES, :] = tu0_ref[...]
        xbuf[0:SUBLANES, :] = tx0_ref[...]

    h = x_ref[...]
    n_bf = _rms(h, nw_ref[...]).astype(BF16)

    cc = _dot(n_bf, win_ref[:, OFF_CC:OFF_CC + D_MODEL])
    cx = _dot(n_bf, win_ref[:, OFF_CX:OFF_CX + D_MODEL])
    ubuf[SUBLANES:SUBLANES + rows, :] = cc * cx
    conv = cw_ref[0:1, :] * ubuf[SUBLANES - 2:SUBLANES - 2 + rows, :]
    conv = conv + cw_ref[1:2, :] * ubuf[SUBLANES - 1:SUBLANES - 1 + rows, :]
    conv = conv + cw_ref[2:3, :] * ubuf[SUBLANES:SUBLANES + rows, :]
    yc = _dot(n_bf, win_ref[:, OFF_CB:OFF_CB + D_MODEL]) * conv
    for g in range(CONV_HEADS):
        sl = slice(g * LANES, (g + 1) * LANES)
        ymix[:, sl] = _rms(yc[:, sl], cnw_ref[:, sl]).astype(BF16)
    ubuf[0:SUBLANES, :] = ubuf[rows:rows + SUBLANES, :]

    xbuf[SUBLANES:SUBLANES + rows, :] = _dot(n_bf, win_ref[:, OFF_XBC:OFF_XBC + D_XBC])
    acc = xb_ref[...] + xw_ref[0:1, :] * xbuf[SUBLANES - 3:SUBLANES - 3 + rows, :]
    for k in range(1, SSD_CONV):
        acc = acc + xw_ref[k:k + 1, :] * xbuf[SUBLANES - 3 + k:SUBLANES - 3 + k + rows, :]
    xc[...] = _silu(acc)
    xbuf[0:SUBLANES, :] = xbuf[rows:rows + SUBLANES, :]

    zs[...] = _silu(_dot(n_bf, win_ref[:, OFF_Z:OFF_Z + D_SSD]))
    dt_raw = _dot(n_bf, win_ref[:, OFF_DT:OFF_DT + LANES]) + dtb_ref[...]
    dt = jnp.maximum(dt_raw, 0.0) + jnp.log(1.0 + jnp.exp(-jnp.abs(dt_raw)))
    if n_inert:
        r_all = lax.broadcasted_iota(I32, (rows, LANES), 0)
        dt = jnp.where(r_all >= n_inert, dt, 0.0)
    a_neg = -jnp.exp(alog_ref[...])
    da = dt * a_neg

    row = lax.broadcasted_iota(I32, (CHUNK, CHUNK), 0)
    lane = lax.broadcasted_iota(I32, (CHUNK, CHUNK), 1)
    causal = row >= lane
    tri = causal.astype(F32)
    left = lane < SSD_HEAD_DIM

    def expand_pair(v, p):
        return jnp.where(left, v[:, 2 * p:2 * p + 1], v[:, 2 * p + 1:2 * p + 2])

    for c in range(n_chunks):
        r0 = c * CHUNK
        rs = slice(r0, r0 + CHUNK)
        dt_c = dt[rs, :]
        a_cs = _dot_exact(tri, da[rs, :])
        a_cs_t = a_cs.T
        a_last = a_cs[CHUNK - 1:CHUNK, :]
        e_cs = jnp.exp(a_cs)
        e_end = jnp.exp(a_last - a_cs)
        e_last = jnp.exp(a_last)
        for g in range(SSD_GROUPS):
            b_g = xc[rs, D_SSD + g * D_STATE:D_SSD + (g + 1) * D_STATE]
            c_g = xc[rs, D_SSD + SSD_GROUPS * D_STATE + g * D_STATE:
                     D_SSD + SSD_GROUPS * D_STATE + (g + 1) * D_STATE].astype(BF16)
            b_gt = b_g.T.astype(BF16)
            cb = _dot(c_g, b_gt)
            gsl = slice(g * 2 * LANES, (g + 1) * 2 * LANES)
            y_off = _dot(c_g, state[:, gsl].astype(BF16))
            xdt_e = []
            for q in range(2):
                p = 2 * g + q
                psl = slice(p * LANES, (p + 1) * LANES)
                xs_p = xc[rs, psl]
                xdt_p = xs_p * expand_pair(dt_c, p)
                m = []
                for hh in range(2):
                    hd = 2 * p + hh
                    seg = a_cs[:, hd:hd + 1] - a_cs_t[hd:hd + 1, :]
                    dec = jnp.where(causal, jnp.exp(jnp.where(causal, seg, 0.0)), 0.0)
                    m.append((cb * dec).astype(BF16))
                lhs = jnp.concatenate(m, axis=1)
                rhs = jnp.concatenate([jnp.where(left, xdt_p, 0.0), jnp.where(left, 0.0, xdt_p)],
                                      axis=0).astype(BF16)
                y_p = _dot(lhs, rhs) + y_off[:, q * LANES:(q + 1) * LANES] * expand_pair(e_cs, p)
                y_p = y_p + dvec_ref[:, psl] * xs_p
                xdt_e.append((xdt_p * expand_pair(e_end, p)).astype(BF16))
                xc[rs, psl] = y_p * zs[rs, psl]
            dec_row = jnp.concatenate([expand_pair(e_last, 2 * g), expand_pair(e_last, 2 * g + 1)], axis=1)
            upd = _dot(b_gt, jnp.concatenate(xdt_e, axis=1))
            state[:, gsl] = dec_row * state[:, gsl] + upd
            ymix[rs, D_MODEL + g * 2 * LANES:D_MODEL + (g + 1) * 2 * LANES] = _rms(
                xc[rs, gsl], snw_ref[:, gsl]).astype(BF16)

    h_ref[...] = h + _dot(ymix[...], wout_ref[...])
    st_ref[...] = state[...]
    tu_ref[...] = ubuf[0:SUBLANES, :]
    tx_ref[...] = xbuf[0:SUBLANES, :]


def _mixer(x, st0, tu0, tx0, wts, *, rows, n_inert):
    bsz, length, _ = x.shape
    n_tiles = length // rows
    const = lambda shp: pl.BlockSpec(shp, lambda b, i: (0,) * len(shp))
    (nw, win, cw, cnw, xw, xb, dtb, alog, dvec, snw, wout) = wts
    kern = functools.partial(_mixer_kernel, rows=rows, n_inert=n_inert)
    return pl.pallas_call(
        kern,
        grid=(bsz, n_tiles),
        in_specs=[
            pl.BlockSpec((None, rows, D_MODEL), lambda b, i: (b, i, 0)),
            const((D_STATE, D_SSD)), const((SUBLANES, D_MODEL)), const((SUBLANES, D_XBC)),
            const((1, D_MODEL)), const((D_MODEL, D_IN_PAD)), const((SUBLANES, D_MODEL)), const((1, D_MODEL)),
            const((SUBLANES, D_XBC)), const((1, D_XBC)), const((1, LANES)), const((1, LANES)),
            const((1, D_SSD)), const((1, D_SSD)), const((2 * D_MODEL, D_MODEL)),
        ],
        out_specs=[
            pl.BlockSpec((None, rows, D_MODEL), lambda b, i: (b, i, 0)),
            pl.BlockSpec((None, D_STATE, D_SSD), lambda b, i: (b, 0, 0)),
            pl.BlockSpec((None, SUBLANES, D_MODEL), lambda b, i: (b, 0, 0)),
            pl.BlockSpec((None, SUBLANES, D_XBC), lambda b, i: (b, 0, 0)),
        ],
        out_shape=[
            jax.ShapeDtypeStruct((bsz, length, D_MODEL), F32),
            jax.ShapeDtypeStruct((bsz, D_STATE, D_SSD), F32),
            jax.ShapeDtypeStruct((bsz, SUBLANES, D_MODEL), F32),
            jax.ShapeDtypeStruct((bsz, SUBLANES, D_XBC), F32),
        ],
        scratch_shapes=[
            pltpu.VMEM((D_STATE, D_SSD), F32),
            pltpu.VMEM((rows + SUBLANES, D_MODEL), F32),
            pltpu.VMEM((rows + SUBLANES, D_XBC), F32),
            pltpu.VMEM((rows, D_XBC), F32),
            pltpu.VMEM((rows, D_SSD), F32),
            pltpu.VMEM((rows, 2 * D_MODEL), BF16),
        ],
        compiler_params=pltpu.CompilerParams(
            dimension_semantics=("arbitrary", "arbitrary"), vmem_limit_bytes=VMEM_LIMIT),
        name="mixer",
    )(x, st0, tu0, tx0, nw, win, cw, cnw, xw, xb, dtb, alog, dvec, snw, wout)


def _mixer_weights(norm_mix_w, w_in, conv_short_w, conv_norm_w, ssd_conv_w, ssd_conv_b, ssd_dt_bias,
                   ssd_a_log, ssd_d, ssd_norm_w, w_out):
    pad_rows = lambda w: jnp.pad(w, ((0, SUBLANES - w.shape[0]), (0, 0)))
    pad_lanes = lambda v: jnp.pad(v, (0, LANES - v.shape[0]))[None, :]
    win = jnp.pad(w_in, ((0, 0), (0, D_IN_PAD - w_in.shape[1]))).astype(BF16)
    return (norm_mix_w[None, :], win, pad_rows(conv_short_w), conv_norm_w[None, :], pad_rows(ssd_conv_w),
            ssd_conv_b[None, :], pad_lanes(ssd_dt_bias), pad_lanes(ssd_a_log),
            jnp.repeat(ssd_d, SSD_HEAD_DIM)[None, :], ssd_norm_w[None, :], w_out.astype(BF16))


def _run_mixer(x, meta_tokens, wts):
    zeros = lambda *s: jnp.zeros(s, F32)
    meta = jnp.concatenate([zeros(CHUNK - N_META, D_MODEL), meta_tokens.astype(F32)], axis=0)[None]
    _, st0, tu0, tx0 = _mixer(meta, zeros(D_STATE, D_SSD), zeros(SUBLANES, D_MODEL), zeros(SUBLANES, D_XBC),
                              wts, rows=CHUNK, n_inert=CHUNK - N_META)
    h, _, _, _ = _mixer(x, st0[0], tu0[0], tx0[0], wts, rows=MIX_ROWS, n_inert=0)
    return h


def _row_tiles_store(ref, val, rows):
    for j in range(LANES_PER_ROW):
        ref[pl.ds(j, rows, stride=SUBLANES), :] = val[:, j * LANES:(j + 1) * LANES]


def _row_tiles_load(ref, first_row, rows):
    return jnp.concatenate(
        [ref[pl.ds(first_row * SUBLANES + j, rows, stride=SUBLANES), :] for j in range(LANES_PER_ROW)], axis=1)


def _router_kernel(h_ref, nw_ref, rw_ref, rb_ref, xn_ref, idx_ref, rank_ref, gate_ref, cnt_ref, carry, *, rows):
    @pl.when(pl.program_id(0) == 0)
    def _():
        carry[...] = jnp.zeros_like(carry)

    xn = _rms(h_ref[...], nw_ref[...])
    _row_tiles_store(xn_ref, xn, rows)

    lane = lax.broadcasted_iota(I32, (rows, LANES), 1)
    logits = _dot_exact(xn, rw_ref[...]) + rb_ref[...]
    cur = jnp.where(lane < N_EXPERTS, logits, -jnp.inf)
    idxs, vals = [], []
    for _ in range(TOP_K):
        m = jnp.max(cur, axis=1, keepdims=True)
        ik = jnp.min(jnp.where(cur == m, lane, LANES), axis=1, keepdims=True)
        cur = jnp.where(lane == ik, -jnp.inf, cur)
        idxs.append(ik)
        vals.append(m)
    es = [jnp.exp(v - vals[0]) for v in vals]
    den = es[0] + es[1] + es[2] + es[3]
    gates = [e / den for e in es]

    onehot = jnp.zeros((rows, LANES), F32)
    for ik in idxs:
        onehot = onehot + (lane == ik).astype(F32)
    r_i = lax.broadcasted_iota(I32, (rows, rows), 0)
    c_i = lax.broadcasted_iota(I32, (rows, rows), 1)
    before = (r_i > c_i).astype(BF16)
    rankmat = _dot(before, onehot.astype(BF16)) + carry[...]
    carry[...] = carry[...] + jnp.sum(onehot, axis=0, keepdims=True)
    ranks = [jnp.sum(jnp.where(lane == ik, rankmat, 0.0), axis=1, keepdims=True) for ik in idxs]

    def pack(cols):
        m = jnp.zeros((rows, LANES), F32)
        for k, col in enumerate(cols):
            m = jnp.where(lane == k, col, m)
        return m

    idx_ref[...] = pack([ik.astype(F32) for ik in idxs]).T[0:SUBLANES, :].astype(I32)
    rank_ref[...] = pack(ranks).T[0:SUBLANES, :].astype(I32)
    gate_ref[...] = pack(gates)
    cnt_ref[...] = jnp.broadcast_to(carry[...], (SUBLANES, LANES))


def _router(h2, nw, rw, rb):
    n_tok = h2.shape[0]
    rows = ROUTE_ROWS
    const = lambda shp: pl.BlockSpec(shp, lambda i: (0,) * len(shp))
    return pl.pallas_call(
        functools.partial(_router_kernel, rows=rows),
        grid=(n_tok // rows,),
        in_specs=[pl.BlockSpec((rows, D_MODEL), lambda i: (i, 0)), const((1, D_MODEL)),
                  const((D_MODEL, LANES)), const((1, LANES))],
        out_specs=[
            pl.BlockSpec((rows * SUBLANES, LANES), lambda i: (i, 0)),
            pl.BlockSpec((SUBLANES, rows), lambda i: (0, i)),
            pl.BlockSpec((SUBLANES, rows), lambda i: (0, i)),
            pl.BlockSpec((rows, LANES), lambda i: (i, 0)),
            const((SUBLANES, LANES)),
        ],
        out_shape=[
            jax.ShapeDtypeStruct((n_tok * SUBLANES, LANES), F32),
            jax.ShapeDtypeStruct((SUBLANES, n_tok), I32),
            jax.ShapeDtypeStruct((SUBLANES, n_tok), I32),
            jax.ShapeDtypeStruct((n_tok, LANES), F32),
            jax.ShapeDtypeStruct((SUBLANES, LANES), F32),
        ],
        scratch_shapes=[pltpu.VMEM((1, LANES), F32)],
        compiler_params=pltpu.CompilerParams(dimension_semantics=("arbitrary",), vmem_limit_bytes=VMEM_LIMIT),
        name="router",
    )(h2, nw, rw, rb)


def _dest_kernel(start_ref, idx_ref, rank_ref, dest_ref):
    idx = idx_ref[...]
    dest = rank_ref[...]
    for e in range(N_EXPERTS):
        dest = dest + jnp.where(idx == e, start_ref[e], 0)
    dest_ref[...] = dest


def _dest(pad_start, idx_t, rank_t):
    n_tok = idx_t.shape[1]
    cols = min(n_tok, 4096)
    spec = pl.BlockSpec((SUBLANES, cols), lambda i, s: (0, i))
    return pl.pallas_call(
        _dest_kernel,
        grid_spec=pltpu.PrefetchScalarGridSpec(
            num_scalar_prefetch=1, grid=(n_tok // cols,), in_specs=[spec, spec], out_specs=spec),
        out_shape=jax.ShapeDtypeStruct((SUBLANES, n_tok), I32),
        name="dest",
    )(pad_start, idx_t, rank_t)


def _pipelined_row_copies(n, make_copy):
    def start_only(j, c):
        make_copy(j).start()
        return c

    def wait_then_start(j, c):
        make_copy(j - DMA_SLOTS).wait()
        make_copy(j).start()
        return c

    def wait_only(j, c):
        make_copy(j).wait()
        return c

    head = jnp.minimum(n, DMA_SLOTS)
    lax.fori_loop(0, head, start_only, 0)
    lax.fori_loop(head, n, wait_then_start, 0)
    lax.fori_loop(jnp.maximum(n - DMA_SLOTS, 0), n, wait_only, 0)


def _dispatch_kernel(zstart_ref, zcount_ref, dest_ref, xn_hbm, zero_hbm, xs_hbm, sem, *, tokens):
    i = pl.program_id(0)
    base = i * tokens

    def token_copy(j):
        return pltpu.make_async_copy(xn_hbm.at[base + (j >> 2)], xs_hbm.at[dest_ref[j]],
                                     sem.at[j & (DMA_SLOTS - 1)])

    _pipelined_row_copies(tokens * TOP_K, token_copy)

    @pl.when(i == pl.num_programs(0) - 1)
    def _():
        def fill_segment(s, c):
            first = zstart_ref[s]

            def zero_copy(j):
                return pltpu.make_async_copy(zero_hbm.at[0], xs_hbm.at[first + j], sem.at[j & (DMA_SLOTS - 1)])

            _pipelined_row_copies(zcount_ref[s], zero_copy)
            return c

        lax.fori_loop(0, N_EXPERTS + 1, fill_segment, 0)


def _dispatch(zstart, zcount, dest_flat, xn3, n_rows):
    n_tok = xn3.shape[0]
    tokens = DISPATCH_ROWS
    zero_row = jnp.zeros((1, SUBLANES, LANES), F32)
    return pl.pallas_call(
        functools.partial(_dispatch_kernel, tokens=tokens),
        grid_spec=pltpu.PrefetchScalarGridSpec(
            num_scalar_prefetch=2, grid=(n_tok // tokens,),
            in_specs=[pl.BlockSpec((tokens * TOP_K,), lambda i, a, b: (i,), memory_space=pltpu.SMEM),
                      pl.BlockSpec(memory_space=pl.ANY), pl.BlockSpec(memory_space=pl.ANY)],
            out_specs=pl.BlockSpec(memory_space=pl.ANY),
            scratch_shapes=[pltpu.SemaphoreType.DMA((DMA_SLOTS,))]),
        out_shape=jax.ShapeDtypeStruct((n_rows, SUBLANES, LANES), F32),
        compiler_params=pltpu.CompilerParams(dimension_semantics=("arbitrary",)),
        name="dispatch",
    )(zstart, zcount, dest_flat, xn3, zero_row)


def _expert_kernel(be_ref, nu_ref, xs_ref, wgu_ref, bgu_ref, wd_ref, bd_ref, ys_ref, *, rows):
    b = pl.program_id(0)

    @pl.when(b < nu_ref[0])
    def _():
        x = _row_tiles_load(xs_ref, 0, rows).astype(BF16)
        gu = _dot(x, wgu_ref[...]) + bgu_ref[...]
        glu = jnp.minimum(gu[:, :D_FF], SWIGLU_LIMIT)
        lin = jnp.clip(gu[:, D_FF:], -SWIGLU_LIMIT, SWIGLU_LIMIT)
        act = glu * (1.0 / (1.0 + jnp.exp(-SWIGLU_ALPHA * glu))) * (lin + 1.0)
        y = _dot(act.astype(BF16), wd_ref[...]) + bd_ref[...]
        _row_tiles_store(ys_ref, y, rows)

    @pl.when(b >= nu_ref[0])
    def _():
        ys_ref[...] = jnp.zeros_like(ys_ref)


def _experts(blk_expert, n_used, xs2, wgu, bgu, wd, bd):
    rows = MOE_BLOCK
    n_blocks = xs2.shape[0] // (rows * SUBLANES)
    by_expert = lambda shp: pl.BlockSpec((None,) + shp, lambda b, be, nu: (be[b], 0, 0))
    return pl.pallas_call(
        functools.partial(_expert_kernel, rows=rows),
        grid_spec=pltpu.PrefetchScalarGridSpec(
            num_scalar_prefetch=2, grid=(n_blocks,),
            in_specs=[pl.BlockSpec((rows * SUBLANES, LANES), lambda b, be, nu: (jnp.minimum(b, nu[0] - 1), 0)),
                      by_expert((D_MODEL, 2 * D_FF)), by_expert((1, 2 * D_FF)),
                      by_expert((D_FF, D_MODEL)), by_expert((1, D_MODEL))],
            out_specs=pl.BlockSpec((rows * SUBLANES, LANES), lambda b, be, nu: (b, 0))),
        out_shape=jax.ShapeDtypeStruct(xs2.shape, F32),
        compiler_params=pltpu.CompilerParams(dimension_semantics=("arbitrary",), vmem_limit_bytes=VMEM_LIMIT),
        name="experts",
    )(blk_expert, n_used, xs2, wgu, bgu, wd, bd)


def _combine_kernel(dest_ref, h_ref, gate_ref, fw_ref, ys_hbm, out_ref, ybuf, sem, *, tokens):
    def row_copy(j):
        slot_row = pl.multiple_of(((j & (TOP_K - 1)) * tokens + (j >> 2)) * SUBLANES, SUBLANES)
        return pltpu.make_async_copy(ys_hbm.at[dest_ref[j]], ybuf.at[pl.ds(slot_row, SUBLANES), :],
                                     sem.at[j & (DMA_SLOTS - 1)])

    _pipelined_row_copies(tokens * TOP_K, row_copy)

    acc = h_ref[...]
    gate = gate_ref[...]
    for k in range(TOP_K):
        acc = acc + gate[:, k:k + 1] * _row_tiles_load(ybuf, k * tokens, tokens)
    out_ref[...] = _rms(acc, fw_ref[...])


def _combine(dest_flat, h2, gates, fw, ys3):
    n_tok = h2.shape[0]
    tokens = DISPATCH_ROWS
    return pl.pallas_call(
        functools.partial(_combine_kernel, tokens=tokens),
        grid=(n_tok // tokens,),
        in_specs=[pl.BlockSpec((tokens * TOP_K,), lambda i: (i,), memory_space=pltpu.SMEM),
                  pl.BlockSpec((tokens, D_MODEL), lambda i: (i, 0)),
                  pl.BlockSpec((tokens, LANES), lambda i: (i, 0)),
                  pl.BlockSpec((1, D_MODEL), lambda i: (0, 0)),
                  pl.BlockSpec(memory_space=pl.ANY)],
        out_specs=pl.BlockSpec((tokens, D_MODEL), lambda i: (i, 0)),
        out_shape=jax.ShapeDtypeStruct((n_tok, D_MODEL), F32),
        scratch_shapes=[pltpu.VMEM((TOP_K * tokens * SUBLANES, LANES), F32),
                        pltpu.SemaphoreType.DMA((DMA_SLOTS,))],
        compiler_params=pltpu.CompilerParams(dimension_semantics=("arbitrary",)),
        name="combine",
    )(dest_flat, h2, gates, fw, ys3)


def _moe(h2, norm_ffn_w, router_w, router_b, w_gate_up, b_gate_up, w_down, b_down, norm_final_w):
    n_tok = h2.shape[0]
    n_assign = n_tok * TOP_K
    n_blocks = -(-(n_assign + N_EXPERTS * (MOE_BLOCK - 1)) // MOE_BLOCK)
    n_rows = n_blocks * MOE_BLOCK

    rw = jnp.pad(router_w, ((0, 0), (0, LANES - N_EXPERTS)))
    rb = jnp.pad(router_b, (0, LANES - N_EXPERTS))[None, :]
    xn2, idx_t, rank_t, gates, cnt = _router(h2, norm_ffn_w[None, :], rw, rb)

    counts = cnt[0, :N_EXPERTS].astype(I32)
    padded = ((counts + MOE_BLOCK - 1) // MOE_BLOCK) * MOE_BLOCK
    pad_end = jnp.cumsum(padded)
    pad_start = pad_end - padded
    blk_expert = jnp.minimum(
        jnp.searchsorted(pad_end, jnp.arange(n_blocks, dtype=I32) * MOE_BLOCK, side='right'),
        N_EXPERTS - 1).astype(I32)
    n_used = (pad_end[-1:] // MOE_BLOCK).astype(I32)
    zstart = jnp.concatenate([pad_start + counts, pad_end[-1:]]).astype(I32)
    zcount = jnp.concatenate([padded - counts, n_rows - pad_end[-1:]]).astype(I32)

    dest_t = _dest(pad_start.astype(I32), idx_t, rank_t)
    dest_flat = dest_t[:TOP_K].T.reshape(-1)

    xs3 = _dispatch(zstart, zcount, dest_flat, xn2.reshape(n_tok, SUBLANES, LANES), n_rows)
    ys2 = _experts(blk_expert, n_used, xs3.reshape(n_rows * SUBLANES, LANES),
                   w_gate_up.astype(BF16), b_gate_up[:, None, :], w_down.astype(BF16), b_down[:, None, :])
    return _combine(dest_flat, h2, gates, norm_final_w[None, :], ys2.reshape(n_rows, SUBLANES, LANES))


def kernel(x, meta_tokens, norm_mix_w, w_in, conv_short_w, conv_norm_w, ssd_conv_w, ssd_conv_b, ssd_dt_bias,
           ssd_a_log, ssd_d, ssd_norm_w, w_out, norm_ffn_w, router_w, router_b, w_gate_up, b_gate_up, w_down,
           b_down, norm_final_w):
    bsz, seq, _ = x.shape
    wts = _mixer_weights(norm_mix_w[0], w_in[0], conv_short_w[0], conv_norm_w[0], ssd_conv_w[0], ssd_conv_b[0],
                         ssd_dt_bias[0], ssd_a_log[0], ssd_d[0], ssd_norm_w[0], w_out[0])
    h = _run_mixer(x, meta_tokens, wts)
    out = _moe(h.reshape(bsz * seq, D_MODEL), norm_ffn_w[0], router_w[0], router_b[0], w_gate_up[0],
               b_gate_up[0], w_down[0], b_down[0], norm_final_w)
    return out.reshape(bsz, seq, D_MODEL)
```

```python
import functools

import jax
import jax.numpy as jnp
from jax import lax
from jax.experimental import pallas as pl
from jax.experimental.pallas import tpu as pltpu

F32 = jnp.float32
BF16 = jnp.bfloat16
I32 = jnp.int32

D_MODEL = 1024
N_META = 16
CONV_WIDTH = 3
CONV_HEADS = 8
SSD_HEADS = 16
SSD_HEAD_DIM = 64
D_SSD = SSD_HEADS * SSD_HEAD_DIM
SSD_GROUPS = 4
D_STATE = 128
SSD_CONV = 4
CHUNK = 128
N_EXPERTS = 32
TOP_K = 4
D_FF = 1024
SWIGLU_ALPHA = 1.702
SWIGLU_LIMIT = 7.0
EPS = 1e-5

LANES = 128
SUBLANES = 8
VMEM_LIMIT = 56 * 1024 * 1024

OFF_CB, OFF_CC, OFF_CX, OFF_Z, OFF_XBC, OFF_DT = 0, 1024, 2048, 3072, 4096, 6144
D_XBC = D_SSD + 2 * SSD_GROUPS * D_STATE
D_IN_PAD = OFF_DT + LANES
LANES_PER_ROW = D_MODEL // LANES

MIX_ROWS = 256
ROUTE_ROWS = 256
MOE_BLOCK = 256
DISPATCH_ROWS = 512
DMA_SLOTS = 256


def _rms(x, w):
    return x * lax.rsqrt(jnp.mean(x * x, axis=-1, keepdims=True) + EPS) * w


def _silu(x):
    return x * (1.0 / (1.0 + jnp.exp(-x)))


def _dot(a, b):
    return jnp.dot(a, b, preferred_element_type=F32)


def _dot_exact(a, b):
    return jnp.dot(a, b, preferred_element_type=F32, precision=lax.Precision.HIGHEST)


def _mixer_kernel(x_ref, st0_ref, tu0_ref, tx0_ref, nw_ref, win_ref, cw_ref, cnw_ref, xw_ref, xb_ref,
                  dtb_ref, alog_ref, dvec_ref, snw_ref, wout_ref,
                  h_ref, st_ref, tu_ref, tx_ref,
                  state, ubuf, xbuf, xc, zs, ymix, *, rows, n_inert):
    i = pl.program_id(1)
    n_chunks = rows // CHUNK

    @pl.when(i == 0)
    def _():
        state[...] = st0_ref[...]
        ubuf[0:SUBLANES, :] = tu0_ref[...]
        xbuf[0:SUBLANES, :] = tx0_ref[...]

    h = x_ref[...]
    n_bf = _rms(h, nw_ref[...]).astype(BF16)

    cc = _dot(n_bf, win_ref[:, OFF_CC:OFF_CC + D_MODEL])
    cx = _dot(n_bf, win_ref[:, OFF_CX:OFF_CX + D_MODEL])
    ubuf[SUBLANES:SUBLANES + rows, :] = cc * cx
    conv = cw_ref[0:1, :] * ubuf[SUBLANES - 2:SUBLANES - 2 + rows, :]
    conv = conv + cw_ref[1:2, :] * ubuf[SUBLANES - 1:SUBLANES - 1 + rows, :]
    conv = conv + cw_ref[2:3, :] * ubuf[SUBLANES:SUBLANES + rows, :]
    yc = _dot(n_bf, win_ref[:, OFF_CB:OFF_CB + D_MODEL]) * conv
    for g in range(CONV_HEADS):
        sl = slice(g * LANES, (g + 1) * LANES)
        ymix[:, sl] = _rms(yc[:, sl], cnw_ref[:, sl]).astype(BF16)
    ubuf[0:SUBLANES, :] = ubuf[rows:rows + SUBLANES, :]

    xbuf[SUBLANES:SUBLANES + rows, :] = _dot(n_bf, win_ref[:, OFF_XBC:OFF_XBC + D_XBC])
    acc = xb_ref[...] + xw_ref[0:1, :] * xbuf[SUBLANES - 3:SUBLANES - 3 + rows, :]
    for k in range(1, SSD_CONV):
        acc = acc + xw_ref[k:k + 1, :] * xbuf[SUBLANES - 3 + k:SUBLANES - 3 + k + rows, :]
    xc[...] = _silu(acc)
    xbuf[0:SUBLANES, :] = xbuf[rows:rows + SUBLANES, :]

    zs[...] = _silu(_dot(n_bf, win_ref[:, OFF_Z:OFF_Z + D_SSD]))
    dt_raw = _dot(n_bf, win_ref[:, OFF_DT:OFF_DT + LANES]) + dtb_ref[...]
    dt = jnp.maximum(dt_raw, 0.0) + jnp.log(1.0 + jnp.exp(-jnp.abs(dt_raw)))
    if n_inert:
        r_all = lax.broadcasted_iota(I32, (rows, LANES), 0)
        dt = jnp.where(r_all >= n_inert, dt, 0.0)
    a_neg = -jnp.exp(alog_ref[...])
    da = dt * a_neg

    row = lax.broadcasted_iota(I32, (CHUNK, CHUNK), 0)
    lane = lax.broadcasted_iota(I32, (CHUNK, CHUNK), 1)
    causal = row >= lane
    tri = causal.astype(F32)
    left = lane < SSD_HEAD_DIM

    def expand_pair(v, p):
        return jnp.where(left, v[:, 2 * p:2 * p + 1], v[:, 2 * p + 1:2 * p + 2])

    for c in range(n_chunks):
        r0 = c * CHUNK
        rs = slice(r0, r0 + CHUNK)
        dt_c = dt[rs, :]
        a_cs = _dot_exact(tri, da[rs, :])
        a_cs_t = a_cs.T
        a_last = a_cs[CHUNK - 1:CHUNK, :]
        e_cs = jnp.exp(a_cs)
        e_end = jnp.exp(a_last - a_cs)
        e_last = jnp.exp(a_last)
        for g in range(SSD_GROUPS):
            b_g = xc[rs, D_SSD + g * D_STATE:D_SSD + (g + 1) * D_STATE]
            c_g = xc[rs, D_SSD + SSD_GROUPS * D_STATE + g * D_STATE:
                     D_SSD + SSD_GROUPS * D_STATE + (g + 1) * D_STATE].astype(BF16)
            b_gt = b_g.T.astype(BF16)
            cb = _dot(c_g, b_gt)
            gsl = slice(g * 2 * LANES, (g + 1) * 2 * LANES)
            y_off = _dot(c_g, state[:, gsl].astype(BF16))
            xdt_e = []
            for q in range(2):
                p = 2 * g + q
                psl = slice(p * LANES, (p + 1) * LANES)
                xs_p = xc[rs, psl]
                xdt_p = xs_p * expand_pair(dt_c, p)
                m = []
                for hh in range(2):
                    hd = 2 * p + hh
                    seg = a_cs[:, hd:hd + 1] - a_cs_t[hd:hd + 1, :]
                    dec = jnp.where(causal, jnp.exp(jnp.where(causal, seg, 0.0)), 0.0)
                    m.append((cb * dec).astype(BF16))
                lhs = jnp.concatenate(m, axis=1)
                rhs = jnp.concatenate([jnp.where(left, xdt_p, 0.0), jnp.where(left, 0.0, xdt_p)],
                                      axis=0).astype(BF16)
                y_p = _dot(lhs, rhs) + y_off[:, q * LANES:(q + 1) * LANES] * expand_pair(e_cs, p)
                y_p = y_p + dvec_ref[:, psl] * xs_p
                xdt_e.append((xdt_p * expand_pair(e_end, p)).astype(BF16))
                xc[rs, psl] = y_p * zs[rs, psl]
            dec_row = jnp.concatenate([expand_pair(e_last, 2 * g), expand_pair(e_last, 2 * g + 1)], axis=1)
            upd = _dot(b_gt, jnp.concatenate(xdt_e, axis=1))
            state[:, gsl] = dec_row * state[:, gsl] + upd
            ymix[rs, D_MODEL + g * 2 * LANES:D_MODEL + (g + 1) * 2 * LANES] = _rms(
                xc[rs, gsl], snw_ref[:, gsl]).astype(BF16)

    h_ref[...] = h + _dot(ymix[...], wout_ref[...])
    st_ref[...] = state[...]
    tu_ref[...] = ubuf[0:SUBLANES, :]
    tx_ref[...] = xbuf[0:SUBLANES, :]


def _mixer(x, st0, tu0, tx0, wts, *, rows, n_inert):
    bsz, length, _ = x.shape
    n_tiles = length // rows
    const = lambda shp: pl.BlockSpec(shp, lambda b, i: (0,) * len(shp))
    (nw, win, cw, cnw, xw, xb, dtb, alog, dvec, snw, wout) = wts
    kern = functools.partial(_mixer_kernel, rows=rows, n_inert=n_inert)
    return pl.pallas_call(
        kern,
        grid=(bsz, n_tiles),
        in_specs=[
            pl.BlockSpec((None, rows, D_MODEL), lambda b, i: (b, i, 0)),
            const((D_STATE, D_SSD)), const((SUBLANES, D_MODEL)), const((SUBLANES, D_XBC)),
            const((1, D_MODEL)), const((D_MODEL, D_IN_PAD)), const((SUBLANES, D_MODEL)), const((1, D_MODEL)),
            const((SUBLANES, D_XBC)), const((1, D_XBC)), const((1, LANES)), const((1, LANES)),
            const((1, D_SSD)), const((1, D_SSD)), const((2 * D_MODEL, D_MODEL)),
        ],
        out_specs=[
            pl.BlockSpec((None, rows, D_MODEL), lambda b, i: (b, i, 0)),
            pl.BlockSpec((None, D_STATE, D_SSD), lambda b, i: (b, 0, 0)),
            pl.BlockSpec((None, SUBLANES, D_MODEL), lambda b, i: (b, 0, 0)),
            pl.BlockSpec((None, SUBLANES, D_XBC), lambda b, i: (b, 0, 0)),
        ],
        out_shape=[
            jax.ShapeDtypeStruct((bsz, length, D_MODEL), F32),
            jax.ShapeDtypeStruct((bsz, D_STATE, D_SSD), F32),
            jax.ShapeDtypeStruct((bsz, SUBLANES, D_MODEL), F32),
            jax.ShapeDtypeStruct((bsz, SUBLANES, D_XBC), F32),
        ],
        scratch_shapes=[
            pltpu.VMEM((D_STATE, D_SSD), F32),
            pltpu.VMEM((rows + SUBLANES, D_MODEL), F32),
            pltpu.VMEM((rows + SUBLANES, D_XBC), F32),
            pltpu.VMEM((rows, D_XBC), F32),
            pltpu.VMEM((rows, D_SSD), F32),
            pltpu.VMEM((rows, 2 * D_MODEL), BF16),
        ],
        compiler_params=pltpu.CompilerParams(
            dimension_semantics=("arbitrary", "arbitrary"), vmem_limit_bytes=VMEM_LIMIT),
        name="mixer",
    )(x, st0, tu0, tx0, nw, win, cw, cnw, xw, xb, dtb, alog, dvec, snw, wout)


def _mixer_weights(norm_mix_w, w_in, conv_short_w, conv_norm_w, ssd_conv_w, ssd_conv_b, ssd_dt_bias,
                   ssd_a_log, ssd_d, ssd_norm_w, w_out):
    pad_rows = lambda w: jnp.pad(w, ((0, SUBLANES - w.shape[0]), (0, 0)))
    pad_lanes = lambda v: jnp.pad(v, (0, LANES - v.shape[0]))[None, :]
    win = jnp.pad(w_in, ((0, 0), (0, D_IN_PAD - w_in.shape[1]))).astype(BF16)
    return (norm_mix_w[None, :], win, pad_rows(conv_short_w), conv_norm_w[None, :], pad_rows(ssd_conv_w),
            ssd_conv_b[None, :], pad_lanes(ssd_dt_bias), pad_lanes(ssd_a_log),
            jnp.repeat(ssd_d, SSD_HEAD_DIM)[None, :], ssd_norm_w[None, :], w_out.astype(BF16))


def _run_mixer(x, meta_tokens, wts):
    zeros = lambda *s: jnp.zeros(s, F32)
    meta = jnp.concatenate([zeros(CHUNK - N_META, D_MODEL), meta_tokens.astype(F32)], axis=0)[None]
    _, st0, tu0, tx0 = _mixer(meta, zeros(D_STATE, D_SSD), zeros(SUBLANES, D_MODEL), zeros(SUBLANES, D_XBC),
                              wts, rows=CHUNK, n_inert=CHUNK - N_META)
    h, _, _, _ = _mixer(x, st0[0], tu0[0], tx0[0], wts, rows=MIX_ROWS, n_inert=0)
    return h


def _row_tiles_store(ref, val, rows):
    for j in range(LANES_PER_ROW):
        ref[pl.ds(j, rows, stride=SUBLANES), :] = val[:, j * LANES:(j + 1) * LANES]


def _row_tiles_load(ref, first_row, rows):
    return jnp.concatenate(
        [ref[pl.ds(first_row * SUBLANES + j, rows, stride=SUBLANES), :] for j in range(LANES_PER_ROW)], axis=1)


def _router_kernel(h_ref, nw_ref, rw_ref, rb_ref, xn_ref, idx_ref, rank_ref, gate_ref, cnt_ref, carry, *, rows):
    @pl.when(pl.program_id(0) == 0)
    def _():
        carry[...] = jnp.zeros_like(carry)

    xn = _rms(h_ref[...], nw_ref[...])
    _row_tiles_store(xn_ref, xn, rows)

    lane = lax.broadcasted_iota(I32, (rows, LANES), 1)
    logits = _dot_exact(xn, rw_ref[...]) + rb_ref[...]
    cur = jnp.where(lane < N_EXPERTS, logits, -jnp.inf)
    idxs, vals = [], []
    for _ in range(TOP_K):
        m = jnp.max(cur, axis=1, keepdims=True)
        ik = jnp.min(jnp.where(cur == m, lane, LANES), axis=1, keepdims=True)
        cur = jnp.where(lane == ik, -jnp.inf, cur)
        idxs.append(ik)
        vals.append(m)
    es = [jnp.exp(v - vals[0]) for v in vals]
    den = es[0] + es[1] + es[2] + es[3]
    gates = [e / den for e in es]

    onehot = jnp.zeros((rows, LANES), F32)
    for ik in idxs:
        onehot = onehot + (lane == ik).astype(F32)
    r_i = lax.broadcasted_iota(I32, (rows, rows), 0)
    c_i = lax.broadcasted_iota(I32, (rows, rows), 1)
    before = (r_i > c_i).astype(BF16)
    rankmat = _dot(before, onehot.astype(BF16)) + carry[...]
    carry[...] = carry[...] + jnp.sum(onehot, axis=0, keepdims=True)
    ranks = [jnp.sum(jnp.where(lane == ik, rankmat, 0.0), axis=1, keepdims=True) for ik in idxs]

    def pack(cols):
        m = jnp.zeros((rows, LANES), F32)
        for k, col in enumerate(cols):
            m = jnp.where(lane == k, col, m)
        return m

    idx_ref[...] = pack([ik.astype(F32) for ik in idxs]).T[0:SUBLANES, :].astype(I32)
    rank_ref[...] = pack(ranks).T[0:SUBLANES, :].astype(I32)
    gate_ref[...] = pack(gates)
    cnt_ref[...] = jnp.broadcast_to(carry[...], (SUBLANES, LANES))


def _router(h2, nw, rw, rb):
    n_tok = h2.shape[0]
    rows = ROUTE_ROWS
    const = lambda shp: pl.BlockSpec(shp, lambda i: (0,) * len(shp))
    return pl.pallas_call(
        functools.partial(_router_kernel, rows=rows),
        grid=(n_tok // rows,),
        in_specs=[pl.BlockSpec((rows, D_MODEL), lambda i: (i, 0)), const((1, D_MODEL)),
                  const((D_MODEL, LANES)), const((1, LANES))],
        out_specs=[
            pl.BlockSpec((rows * SUBLANES, LANES), lambda i: (i, 0)),
            pl.BlockSpec((SUBLANES, rows), lambda i: (0, i)),
            pl.BlockSpec((SUBLANES, rows), lambda i: (0, i)),
            pl.BlockSpec((rows, LANES), lambda i: (i, 0)),
            const((SUBLANES, LANES)),
        ],
        out_shape=[
            jax.ShapeDtypeStruct((n_tok * SUBLANES, LANES), F32),
            jax.ShapeDtypeStruct((SUBLANES, n_tok), I32),
            jax.ShapeDtypeStruct((SUBLANES, n_tok), I32),
            jax.ShapeDtypeStruct((n_tok, LANES), F32),
            jax.ShapeDtypeStruct((SUBLANES, LANES), F32),
        ],
        scratch_shapes=[pltpu.VMEM((1, LANES), F32)],
        compiler_params=pltpu.CompilerParams(dimension_semantics=("arbitrary",), vmem_limit_bytes=VMEM_LIMIT),
        name="router",
    )(h2, nw, rw, rb)


def _dest_kernel(start_ref, idx_ref, rank_ref, dest_ref):
    idx = idx_ref[...]
    dest = rank_ref[...]
    for e in range(N_EXPERTS):
        dest = dest + jnp.where(idx == e, start_ref[e], 0)
    dest_ref[...] = dest


def _dest(pad_start, idx_t, rank_t):
    n_tok = idx_t.shape[1]
    cols = min(n_tok, 4096)
    spec = pl.BlockSpec((SUBLANES, cols), lambda i, s: (0, i))
    return pl.pallas_call(
        _dest_kernel,
        grid_spec=pltpu.PrefetchScalarGridSpec(
            num_scalar_prefetch=1, grid=(n_tok // cols,), in_specs=[spec, spec], out_specs=spec),
        out_shape=jax.ShapeDtypeStruct((SUBLANES, n_tok), I32),
        name="dest",
    )(pad_start, idx_t, rank_t)


def _pipelined_row_copies(n, make_copy):
    def start_only(j, c):
        make_copy(j).start()
        return c

    def wait_then_start(j, c):
        make_copy(j - DMA_SLOTS).wait()
        make_copy(j).start()
        return c

    def wait_only(j, c):
        make_copy(j).wait()
        return c

    head = jnp.minimum(n, DMA_SLOTS)
    lax.fori_loop(0, head, start_only, 0)
    lax.fori_loop(head, n, wait_then_start, 0)
    lax.fori_loop(jnp.maximum(n - DMA_SLOTS, 0), n, wait_only, 0)


def _dispatch_kernel(zstart_ref, zcount_ref, dest_ref, xn_ref, xs_hbm, zero_row, sem, *, tokens):
    i = pl.program_id(0)

    def token_copy(j):
        src_row = pl.multiple_of((j >> 2) * SUBLANES, SUBLANES)
        return pltpu.make_async_copy(xn_ref.at[pl.ds(src_row, SUBLANES), :], xs_hbm.at[dest_ref[j]],
                                     sem.at[j & (DMA_SLOTS - 1)])

    _pipelined_row_copies(tokens * TOP_K, token_copy)

    @pl.when(i == pl.num_programs(0) - 1)
    def _():
        zero_row[...] = jnp.zeros_like(zero_row)

        def fill_segment(s, c):
            first = zstart_ref[s]

            def zero_copy(j):
                return pltpu.make_async_copy(zero_row, xs_hbm.at[first + j], sem.at[j & (DMA_SLOTS - 1)])

            _pipelined_row_copies(zcount_ref[s], zero_copy)
            return c

        lax.fori_loop(0, N_EXPERTS + 1, fill_segment, 0)


def _dispatch(zstart, zcount, dest_flat, xn2, n_rows):
    n_tok = xn2.shape[0] // SUBLANES
    tokens = DISPATCH_ROWS
    return pl.pallas_call(
        functools.partial(_dispatch_kernel, tokens=tokens),
        grid_spec=pltpu.PrefetchScalarGridSpec(
            num_scalar_prefetch=2, grid=(n_tok // tokens,),
            in_specs=[pl.BlockSpec((tokens * TOP_K,), lambda i, a, b: (i,), memory_space=pltpu.SMEM),
                      pl.BlockSpec((tokens * SUBLANES, LANES), lambda i, a, b: (i, 0))],
            out_specs=pl.BlockSpec(memory_space=pl.ANY),
            scratch_shapes=[pltpu.VMEM((SUBLANES, LANES), F32), pltpu.SemaphoreType.DMA((DMA_SLOTS,))]),
        out_shape=jax.ShapeDtypeStruct((n_rows, SUBLANES, LANES), F32),
        compiler_params=pltpu.CompilerParams(dimension_semantics=("arbitrary",)),
        name="dispatch",
    )(zstart, zcount, dest_flat, xn2)


def _expert_kernel(be_ref, nu_ref, xs_ref, wgu_ref, bgu_ref, wd_ref, bd_ref, ys_ref, *, rows):
    b = pl.program_id(0)

    @pl.when(b < nu_ref[0])
    def _():
        x = _row_tiles_load(xs_ref, 0, rows).astype(BF16)
        gu = _dot(x, wgu_ref[...]) + bgu_ref[...]
        glu = jnp.minimum(gu[:, :D_FF], SWIGLU_LIMIT)
        lin = jnp.clip(gu[:, D_FF:], -SWIGLU_LIMIT, SWIGLU_LIMIT)
        act = glu * (1.0 / (1.0 + jnp.exp(-SWIGLU_ALPHA * glu))) * (lin + 1.0)
        y = _dot(act.astype(BF16), wd_ref[...]) + bd_ref[...]
        _row_tiles_store(ys_ref, y, rows)

    @pl.when(b >= nu_ref[0])
    def _():
        ys_ref[...] = jnp.zeros_like(ys_ref)


def _experts(blk_expert, n_used, xs2, wgu, bgu, wd, bd):
    rows = MOE_BLOCK
    n_blocks = xs2.shape[0] // (rows * SUBLANES)
    by_expert = lambda shp: pl.BlockSpec((None,) + shp, lambda b, be, nu: (be[b], 0, 0))
    return pl.pallas_call(
        functools.partial(_expert_kernel, rows=rows),
        grid_spec=pltpu.PrefetchScalarGridSpec(
            num_scalar_prefetch=2, grid=(n_blocks,),
            in_specs=[pl.BlockSpec((rows * SUBLANES, LANES), lambda b, be, nu: (jnp.minimum(b, nu[0] - 1), 0)),
                      by_expert((D_MODEL, 2 * D_FF)), by_expert((1, 2 * D_FF)),
                      by_expert((D_FF, D_MODEL)), by_expert((1, D_MODEL))],
            out_specs=pl.BlockSpec((rows * SUBLANES, LANES), lambda b, be, nu: (b, 0))),
        out_shape=jax.ShapeDtypeStruct(xs2.shape, F32),
        compiler_params=pltpu.CompilerParams(dimension_semantics=("arbitrary",), vmem_limit_bytes=VMEM_LIMIT),
        name="experts",
    )(blk_expert, n_used, xs2, wgu, bgu, wd, bd)


def _combine_kernel(dest_ref, h_ref, gate_ref, fw_ref, ys_hbm, out_ref, ybuf, sem, *, tokens):
    def row_copy(j):
        slot_row = pl.multiple_of(((j & (TOP_K - 1)) * tokens + (j >> 2)) * SUBLANES, SUBLANES)
        return pltpu.make_async_copy(ys_hbm.at[dest_ref[j]], ybuf.at[pl.ds(slot_row, SUBLANES), :],
                                     sem.at[j & (DMA_SLOTS - 1)])

    _pipelined_row_copies(tokens * TOP_K, row_copy)

    acc = h_ref[...]
    gate = gate_ref[...]
    for k in range(TOP_K):
        acc = acc + gate[:, k:k + 1] * _row_tiles_load(ybuf, k * tokens, tokens)
    out_ref[...] = _rms(acc, fw_ref[...])


def _combine(dest_flat, h2, gates, fw, ys3):
    n_tok = h2.shape[0]
    tokens = DISPATCH_ROWS
    return pl.pallas_call(
        functools.partial(_combine_kernel, tokens=tokens),
        grid=(n_tok // tokens,),
        in_specs=[pl.BlockSpec((tokens * TOP_K,), lambda i: (i,), memory_space=pltpu.SMEM),
                  pl.BlockSpec((tokens, D_MODEL), lambda i: (i, 0)),
                  pl.BlockSpec((tokens, LANES), lambda i: (i, 0)),
                  pl.BlockSpec((1, D_MODEL), lambda i: (0, 0)),
                  pl.BlockSpec(memory_space=pl.ANY)],
        out_specs=pl.BlockSpec((tokens, D_MODEL), lambda i: (i, 0)),
        out_shape=jax.ShapeDtypeStruct((n_tok, D_MODEL), F32),
        scratch_shapes=[pltpu.VMEM((TOP_K * tokens * SUBLANES, LANES), F32),
                        pltpu.SemaphoreType.DMA((DMA_SLOTS,))],
        compiler_params=pltpu.CompilerParams(dimension_semantics=("arbitrary",), vmem_limit_bytes=VMEM_LIMIT),
        name="combine",
    )(dest_flat, h2, gates, fw, ys3)


def _moe(h2, norm_ffn_w, router_w, router_b, w_gate_up, b_gate_up, w_down, b_down, norm_final_w):
    n_tok = h2.shape[0]
    n_assign = n_tok * TOP_K
    n_blocks = -(-(n_assign + N_EXPERTS * (MOE_BLOCK - 1)) // MOE_BLOCK)
    n_rows = n_blocks * MOE_BLOCK

    rw = jnp.pad(router_w, ((0, 0), (0, LANES - N_EXPERTS)))
    rb = jnp.pad(router_b, (0, LANES - N_EXPERTS))[None, :]
    xn2, idx_t, rank_t, gates, cnt = _router(h2, norm_ffn_w[None, :], rw, rb)

    counts = cnt[0, :N_EXPERTS].astype(I32)
    padded = ((counts + MOE_BLOCK - 1) // MOE_BLOCK) * MOE_BLOCK
    pad_end = jnp.cumsum(padded)
    pad_start = pad_end - padded
    blk_first_row = jnp.arange(n_blocks, dtype=I32) * MOE_BLOCK
    blk_expert = jnp.minimum(jnp.sum((pad_end[None, :] <= blk_first_row[:, None]).astype(I32), axis=1),
                             N_EXPERTS - 1)
    n_used = (pad_end[-1:] // MOE_BLOCK).astype(I32)
    zstart = jnp.concatenate([pad_start + counts, pad_end[-1:]]).astype(I32)
    zcount = jnp.concatenate([padded - counts, n_rows - pad_end[-1:]]).astype(I32)

    dest_t = _dest(pad_start.astype(I32), idx_t, rank_t)
    dest_flat = dest_t[:TOP_K].T.reshape(-1)

    xs3 = _dispatch(zstart, zcount, dest_flat, xn2, n_rows)
    ys2 = _experts(blk_expert, n_used, xs3.reshape(n_rows * SUBLANES, LANES),
                   w_gate_up.astype(BF16), b_gate_up[:, None, :], w_down.astype(BF16), b_down[:, None, :])
    return _combine(dest_flat, h2, gates, norm_final_w[None, :], ys2.reshape(n_rows, SUBLANES, LANES))


def kernel(x, meta_tokens, norm_mix_w, w_in, conv_short_w, conv_norm_w, ssd_conv_w, ssd_conv_b, ssd_dt_bias,
           ssd_a_log, ssd_d, ssd_norm_w, w_out, norm_ffn_w, router_w, router_b, w_gate_up, b_gate_up, w_down,
           b_down, norm_final_w):
    bsz, seq, _ = x.shape
    wts = _mixer_weights(norm_mix_w[0], w_in[0], conv_short_w[0], conv_norm_w[0], ssd_conv_w[0], ssd_conv_b[0],
                         ssd_dt_bias[0], ssd_a_log[0], ssd_d[0], ssd_norm_w[0], w_out[0])
    h = _run_mixer(x, meta_tokens, wts)
    out = _moe(h.reshape(bsz * seq, D_MODEL), norm_ffn_w[0], router_w[0], router_b[0], w_gate_up[0],
               b_gate_up[0], w_down[0], b_down[0], norm_final_w)
    return out.reshape(bsz, seq, D_MODEL)
```

```python
import functools

import jax
import jax.numpy as jnp
from jax import lax
from jax.experimental import pallas as pl
from jax.experimental.pallas import tpu as pltpu

F32 = jnp.float32
BF16 = jnp.bfloat16
I32 = jnp.int32

D_MODEL = 1024
N_META = 16
CONV_WIDTH = 3
CONV_HEADS = 8
SSD_HEADS = 16
SSD_HEAD_DIM = 64
D_SSD = SSD_HEADS * SSD_HEAD_DIM
SSD_GROUPS = 4
D_STATE = 128
SSD_CONV = 4
CHUNK = 128
N_EXPERTS = 32
TOP_K = 4
D_FF = 1024
SWIGLU_ALPHA = 1.702
SWIGLU_LIMIT = 7.0
EPS = 1e-5

LANES = 128
SUBLANES = 8
VMEM_LIMIT = 56 * 1024 * 1024

OFF_CB, OFF_CC, OFF_CX, OFF_Z, OFF_XBC, OFF_DT = 0, 1024, 2048, 3072, 4096, 6144
D_XBC = D_SSD + 2 * SSD_GROUPS * D_STATE
D_IN_PAD = OFF_DT + LANES
LANES_PER_ROW = D_MODEL // LANES

MIX_ROWS = 256
ROUTE_ROWS = 256
MOE_BLOCK = 256
DISPATCH_ROWS = 512
DMA_SLOTS = 64
ISSUE_UNROLL = 8
CAST_ROWS = 128


def _rms(x, w):
    return x * lax.rsqrt(jnp.mean(x * x, axis=-1, keepdims=True) + EPS) * w


def _silu(x):
    return x * (1.0 / (1.0 + jnp.exp(-x)))


def _dot(a, b):
    return jnp.dot(a, b, preferred_element_type=F32)


def _dot_exact(a, b):
    return jnp.dot(a, b, preferred_element_type=F32, precision=lax.Precision.HIGHEST)


def _mixer_kernel(x_ref, st0_ref, tu0_ref, tx0_ref, nw_ref, win_ref, cw_ref, cnw_ref, xw_ref, xb_ref,
                  dtb_ref, alog_ref, dvec_ref, snw_ref, wout_ref,
                  h_ref, st_ref, tu_ref, tx_ref,
                  state, ubuf, xbuf, xc, zs, ymix, *, rows, n_inert):
    i = pl.program_id(1)
    n_chunks = rows // CHUNK

    @pl.when(i == 0)
    def _():
        state[...] = st0_ref[...]
        ubuf[0:SUBLANES, :] = tu0_ref[...]
        xbuf[0:SUBLANES, :] = tx0_ref[...]

    h = x_ref[...]
    n_bf = _rms(h, nw_ref[...]).astype(BF16)

    cc = _dot(n_bf, win_ref[:, OFF_CC:OFF_CC + D_MODEL])
    cx = _dot(n_bf, win_ref[:, OFF_CX:OFF_CX + D_MODEL])
    ubuf[SUBLANES:SUBLANES + rows, :] = cc * cx
    conv = cw_ref[0:1, :] * ubuf[SUBLANES - 2:SUBLANES - 2 + rows, :]
    conv = conv + cw_ref[1:2, :] * ubuf[SUBLANES - 1:SUBLANES - 1 + rows, :]
    conv = conv + cw_ref[2:3, :] * ubuf[SUBLANES:SUBLANES + rows, :]
    yc = _dot(n_bf, win_ref[:, OFF_CB:OFF_CB + D_MODEL]) * conv
    for g in range(CONV_HEADS):
        sl = slice(g * LANES, (g + 1) * LANES)
        ymix[:, sl] = _rms(yc[:, sl], cnw_ref[:, sl]).astype(BF16)
    ubuf[0:SUBLANES, :] = ubuf[rows:rows + SUBLANES, :]

    xbuf[SUBLANES:SUBLANES + rows, :] = _dot(n_bf, win_ref[:, OFF_XBC:OFF_XBC + D_XBC])
    acc = xb_ref[...] + xw_ref[0:1, :] * xbuf[SUBLANES - 3:SUBLANES - 3 + rows, :]
    for k in range(1, SSD_CONV):
        acc = acc + xw_ref[k:k + 1, :] * xbuf[SUBLANES - 3 + k:SUBLANES - 3 + k + rows, :]
    xc[...] = _silu(acc)
    xbuf[0:SUBLANES, :] = xbuf[rows:rows + SUBLANES, :]

    zs[...] = _silu(_dot(n_bf, win_ref[:, OFF_Z:OFF_Z + D_SSD]))
    dt_raw = _dot(n_bf, win_ref[:, OFF_DT:OFF_DT + LANES]) + dtb_ref[...]
    dt = jnp.maximum(dt_raw, 0.0) + jnp.log(1.0 + jnp.exp(-jnp.abs(dt_raw)))
    if n_inert:
        r_all = lax.broadcasted_iota(I32, (rows, LANES), 0)
        dt = jnp.where(r_all >= n_inert, dt, 0.0)
    a_neg = -jnp.exp(alog_ref[...])
    da = dt * a_neg

    row = lax.broadcasted_iota(I32, (CHUNK, CHUNK), 0)
    lane = lax.broadcasted_iota(I32, (CHUNK, CHUNK), 1)
    causal = row >= lane
    tri = causal.astype(F32)
    left = lane < SSD_HEAD_DIM

    def expand_pair(v, p):
        return jnp.where(left, v[:, 2 * p:2 * p + 1], v[:, 2 * p + 1:2 * p + 2])

    for c in range(n_chunks):
        r0 = c * CHUNK
        rs = slice(r0, r0 + CHUNK)
        dt_c = dt[rs, :]
        a_cs = _dot_exact(tri, da[rs, :])
        a_cs_t = a_cs.T
        a_last = a_cs[CHUNK - 1:CHUNK, :]
        e_cs = jnp.exp(a_cs)
        e_end = jnp.exp(a_last - a_cs)
        e_last = jnp.exp(a_last)
        for g in range(SSD_GROUPS):
            b_g = xc[rs, D_SSD + g * D_STATE:D_SSD + (g + 1) * D_STATE]
            c_g = xc[rs, D_SSD + SSD_GROUPS * D_STATE + g * D_STATE:
                     D_SSD + SSD_GROUPS * D_STATE + (g + 1) * D_STATE].astype(BF16)
            b_gt = b_g.T.astype(BF16)
            cb = _dot(c_g, b_gt)
            gsl = slice(g * 2 * LANES, (g + 1) * 2 * LANES)
            y_off = _dot(c_g, state[:, gsl].astype(BF16))
            xdt_e = []
            for q in range(2):
                p = 2 * g + q
                psl = slice(p * LANES, (p + 1) * LANES)
                xs_p = xc[rs, psl]
                xdt_p = xs_p * expand_pair(dt_c, p)
                m = []
                for hh in range(2):
                    hd = 2 * p + hh
                    seg = a_cs[:, hd:hd + 1] - a_cs_t[hd:hd + 1, :]
                    dec = jnp.where(causal, jnp.exp(jnp.where(causal, seg, 0.0)), 0.0)
                    m.append((cb * dec).astype(BF16))
                lhs = jnp.concatenate(m, axis=1)
                rhs = jnp.concatenate([jnp.where(left, xdt_p, 0.0), jnp.where(left, 0.0, xdt_p)],
                                      axis=0).astype(BF16)
                y_p = _dot(lhs, rhs) + y_off[:, q * LANES:(q + 1) * LANES] * expand_pair(e_cs, p)
                y_p = y_p + dvec_ref[:, psl] * xs_p
                xdt_e.append((xdt_p * expand_pair(e_end, p)).astype(BF16))
                xc[rs, psl] = y_p * zs[rs, psl]
            dec_row = jnp.concatenate([expand_pair(e_last, 2 * g), expand_pair(e_last, 2 * g + 1)], axis=1)
            upd = _dot(b_gt, jnp.concatenate(xdt_e, axis=1))
            state[:, gsl] = dec_row * state[:, gsl] + upd
            ymix[rs, D_MODEL + g * 2 * LANES:D_MODEL + (g + 1) * 2 * LANES] = _rms(
                xc[rs, gsl], snw_ref[:, gsl]).astype(BF16)

    h_ref[...] = h + _dot(ymix[...], wout_ref[...])
    st_ref[...] = state[...]
    tu_ref[...] = ubuf[0:SUBLANES, :]
    tx_ref[...] = xbuf[0:SUBLANES, :]


def _mixer(x, st0, tu0, tx0, wts, *, rows, n_inert):
    bsz, length, _ = x.shape
    n_tiles = length // rows
    const = lambda shp: pl.BlockSpec(shp, lambda b, i: (0,) * len(shp))
    (nw, win, cw, cnw, xw, xb, dtb, alog, dvec, snw, wout) = wts
    kern = functools.partial(_mixer_kernel, rows=rows, n_inert=n_inert)
    return pl.pallas_call(
        kern,
        grid=(bsz, n_tiles),
        in_specs=[
            pl.BlockSpec((None, rows, D_MODEL), lambda b, i: (b, i, 0)),
            const((D_STATE, D_SSD)), const((SUBLANES, D_MODEL)), const((SUBLANES, D_XBC)),
            const((1, D_MODEL)), const((D_MODEL, D_IN_PAD)), const((SUBLANES, D_MODEL)), const((1, D_MODEL)),
            const((SUBLANES, D_XBC)), const((1, D_XBC)), const((1, LANES)), const((1, LANES)),
            const((1, D_SSD)), const((1, D_SSD)), const((2 * D_MODEL, D_MODEL)),
        ],
        out_specs=[
            pl.BlockSpec((None, rows, D_MODEL), lambda b, i: (b, i, 0)),
            pl.BlockSpec((None, D_STATE, D_SSD), lambda b, i: (b, 0, 0)),
            pl.BlockSpec((None, SUBLANES, D_MODEL), lambda b, i: (b, 0, 0)),
            pl.BlockSpec((None, SUBLANES, D_XBC), lambda b, i: (b, 0, 0)),
        ],
        out_shape=[
            jax.ShapeDtypeStruct((bsz, length, D_MODEL), F32),
            jax.ShapeDtypeStruct((bsz, D_STATE, D_SSD), F32),
            jax.ShapeDtypeStruct((bsz, SUBLANES, D_MODEL), F32),
            jax.ShapeDtypeStruct((bsz, SUBLANES, D_XBC), F32),
        ],
        scratch_shapes=[
            pltpu.VMEM((D_STATE, D_SSD), F32),
            pltpu.VMEM((rows + SUBLANES, D_MODEL), F32),
            pltpu.VMEM((rows + SUBLANES, D_XBC), F32),
            pltpu.VMEM((rows, D_XBC), F32),
            pltpu.VMEM((rows, D_SSD), F32),
            pltpu.VMEM((rows, 2 * D_MODEL), BF16),
        ],
        compiler_params=pltpu.CompilerParams(
            dimension_semantics=("arbitrary", "arbitrary"), vmem_limit_bytes=VMEM_LIMIT),
        name="mixer",
    )(x, st0, tu0, tx0, nw, win, cw, cnw, xw, xb, dtb, alog, dvec, snw, wout)


def _mixer_weights(norm_mix_w, w_in, conv_short_w, conv_norm_w, ssd_conv_w, ssd_conv_b, ssd_dt_bias,
                   ssd_a_log, ssd_d, ssd_norm_w, w_out):
    pad_rows = lambda w: jnp.pad(w, ((0, SUBLANES - w.shape[0]), (0, 0)))
    pad_lanes = lambda v: jnp.pad(v, (0, LANES - v.shape[0]))[None, :]
    win = jnp.pad(w_in, ((0, 0), (0, D_IN_PAD - w_in.shape[1]))).astype(BF16)
    return (norm_mix_w[None, :], win, pad_rows(conv_short_w), conv_norm_w[None, :], pad_rows(ssd_conv_w),
            ssd_conv_b[None, :], pad_lanes(ssd_dt_bias), pad_lanes(ssd_a_log),
            jnp.repeat(ssd_d, SSD_HEAD_DIM)[None, :], ssd_norm_w[None, :], w_out.astype(BF16))


def _run_mixer(x, meta_tokens, wts):
    zeros = lambda *s: jnp.zeros(s, F32)
    meta = jnp.concatenate([zeros(CHUNK - N_META, D_MODEL), meta_tokens.astype(F32)], axis=0)[None]
    _, st0, tu0, tx0 = _mixer(meta, zeros(D_STATE, D_SSD), zeros(SUBLANES, D_MODEL), zeros(SUBLANES, D_XBC),
                              wts, rows=CHUNK, n_inert=CHUNK - N_META)
    h, _, _, _ = _mixer(x, st0[0], tu0[0], tx0[0], wts, rows=MIX_ROWS, n_inert=0)
    return h


def _row_tiles_store(ref, val, rows):
    for j in range(LANES_PER_ROW):
        ref[pl.ds(j, rows, stride=SUBLANES), :] = val[:, j * LANES:(j + 1) * LANES]


def _row_tiles_load(ref, first_row, rows):
    return jnp.concatenate(
        [ref[pl.ds(first_row * SUBLANES + j, rows, stride=SUBLANES), :] for j in range(LANES_PER_ROW)], axis=1)


def _router_kernel(h_ref, nw_ref, rw_ref, rb_ref, xn_ref, idx_ref, rank_ref, gate_ref, cnt_ref, carry, *, rows):
    @pl.when(pl.program_id(0) == 0)
    def _():
        carry[...] = jnp.zeros_like(carry)

    xn = _rms(h_ref[...], nw_ref[...])
    _row_tiles_store(xn_ref, xn, rows)

    lane = lax.broadcasted_iota(I32, (rows, LANES), 1)
    logits = _dot_exact(xn, rw_ref[...]) + rb_ref[...]
    cur = jnp.where(lane < N_EXPERTS, logits, -jnp.inf)
    idxs, vals = [], []
    for _ in range(TOP_K):
        m = jnp.max(cur, axis=1, keepdims=True)
        ik = jnp.min(jnp.where(cur == m, lane, LANES), axis=1, keepdims=True)
        cur = jnp.where(lane == ik, -jnp.inf, cur)
        idxs.append(ik)
        vals.append(m)
    es = [jnp.exp(v - vals[0]) for v in vals]
    den = es[0] + es[1] + es[2] + es[3]
    gates = [e / den for e in es]

    onehot = jnp.zeros((rows, LANES), F32)
    for ik in idxs:
        onehot = onehot + (lane == ik).astype(F32)
    r_i = lax.broadcasted_iota(I32, (rows, rows), 0)
    c_i = lax.broadcasted_iota(I32, (rows, rows), 1)
    before = (r_i > c_i).astype(BF16)
    rankmat = _dot(before, onehot.astype(BF16)) + carry[...]
    carry[...] = carry[...] + jnp.sum(onehot, axis=0, keepdims=True)
    ranks = [jnp.sum(jnp.where(lane == ik, rankmat, 0.0), axis=1, keepdims=True) for ik in idxs]

    def pack(cols):
        m = jnp.zeros((rows, LANES), F32)
        for k, col in enumerate(cols):
            m = jnp.where(lane == k, col, m)
        return m

    idx_ref[...] = pack([ik.astype(F32) for ik in idxs]).T[0:SUBLANES, :].astype(I32)
    rank_ref[...] = pack(ranks).T[0:SUBLANES, :].astype(I32)
    gate_ref[...] = pack(gates)
    cnt_ref[...] = jnp.broadcast_to(carry[...], (SUBLANES, LANES))


def _router(h2, nw, rw, rb):
    n_tok = h2.shape[0]
    rows = ROUTE_ROWS
    const = lambda shp: pl.BlockSpec(shp, lambda i: (0,) * len(shp))
    return pl.pallas_call(
        functools.partial(_router_kernel, rows=rows),
        grid=(n_tok // rows,),
        in_specs=[pl.BlockSpec((rows, D_MODEL), lambda i: (i, 0)), const((1, D_MODEL)),
                  const((D_MODEL, LANES)), const((1, LANES))],
        out_specs=[
            pl.BlockSpec((rows * SUBLANES, LANES), lambda i: (i, 0)),
            pl.BlockSpec((SUBLANES, rows), lambda i: (0, i)),
            pl.BlockSpec((SUBLANES, rows), lambda i: (0, i)),
            pl.BlockSpec((rows, LANES), lambda i: (i, 0)),
            const((SUBLANES, LANES)),
        ],
        out_shape=[
            jax.ShapeDtypeStruct((n_tok * SUBLANES, LANES), F32),
            jax.ShapeDtypeStruct((SUBLANES, n_tok), I32),
            jax.ShapeDtypeStruct((SUBLANES, n_tok), I32),
            jax.ShapeDtypeStruct((n_tok, LANES), F32),
            jax.ShapeDtypeStruct((SUBLANES, LANES), F32),
        ],
        scratch_shapes=[pltpu.VMEM((1, LANES), F32)],
        compiler_params=pltpu.CompilerParams(dimension_semantics=("arbitrary",), vmem_limit_bytes=VMEM_LIMIT),
        name="router",
    )(h2, nw, rw, rb)


def _dest_kernel(start_ref, idx_ref, rank_ref, dest_ref):
    idx = idx_ref[...]
    dest = rank_ref[...]
    for e in range(N_EXPERTS):
        dest = dest + jnp.where(idx == e, start_ref[e], 0)
    dest_ref[...] = dest


def _dest(pad_start, idx_t, rank_t):
    n_tok = idx_t.shape[1]
    cols = min(n_tok, 4096)
    spec = pl.BlockSpec((SUBLANES, cols), lambda i, s: (0, i))
    return pl.pallas_call(
        _dest_kernel,
        grid_spec=pltpu.PrefetchScalarGridSpec(
            num_scalar_prefetch=1, grid=(n_tok // cols,), in_specs=[spec, spec], out_specs=spec),
        out_shape=jax.ShapeDtypeStruct((SUBLANES, n_tok), I32),
        name="dest",
    )(pad_start, idx_t, rank_t)


def _pipelined_row_copies(n, make_copy):
    def start_only(j, c):
        make_copy(j).start()
        return c

    def wait_then_start(j, c):
        make_copy(j - DMA_SLOTS).wait()
        make_copy(j).start()
        return c

    def wait_only(j, c):
        make_copy(j).wait()
        return c

    head = jnp.minimum(n, DMA_SLOTS)
    lax.fori_loop(0, head, start_only, 0)
    lax.fori_loop(head, n, wait_then_start, 0)
    lax.fori_loop(jnp.maximum(n - DMA_SLOTS, 0), n, wait_only, 0)


def _start_row_burst(n_rows, make_copy):
    def body(g, c):
        for u in range(ISSUE_UNROLL):
            make_copy(g * ISSUE_UNROLL + u).start(priority=u % 2)
        return c

    lax.fori_loop(0, n_rows // ISSUE_UNROLL, body, 0)


def _tile_rows(row):
    return pl.ds(pl.multiple_of(row * SUBLANES, SUBLANES), SUBLANES)


def _dispatch_kernel(zstart_ref, zcount_ref, dest_ref, xn_ref, xs_hbm, zero_row, sem, zsem, *, tokens):
    i = pl.program_id(0)

    def token_copy(j):
        return pltpu.make_async_copy(xn_ref.at[_tile_rows(j >> 2), :], xs_hbm.at[_tile_rows(dest_ref[j]), :],
                                     sem.at[0])

    _start_row_burst(tokens * TOP_K, token_copy)
    for _ in range(TOP_K):
        pltpu.make_async_copy(xn_ref, xs_hbm.at[pl.ds(0, tokens * SUBLANES), :], sem.at[0]).wait()

    @pl.when(i == pl.num_programs(0) - 1)
    def _():
        zero_row[...] = jnp.zeros_like(zero_row)

        def fill_segment(s, c):
            first = zstart_ref[s]

            def zero_copy(j):
                return pltpu.make_async_copy(zero_row, xs_hbm.at[_tile_rows(first + j), :],
                                             zsem.at[j & (DMA_SLOTS - 1)])

            _pipelined_row_copies(zcount_ref[s], zero_copy)
            return c

        lax.fori_loop(0, N_EXPERTS + 1, fill_segment, 0)


def _dispatch(zstart, zcount, dest_flat, xn2, n_rows):
    n_tok = xn2.shape[0] // SUBLANES
    tokens = DISPATCH_ROWS
    return pl.pallas_call(
        functools.partial(_dispatch_kernel, tokens=tokens),
        grid_spec=pltpu.PrefetchScalarGridSpec(
            num_scalar_prefetch=2, grid=(n_tok // tokens,),
            in_specs=[pl.BlockSpec((tokens * TOP_K,), lambda i, a, b: (i,), memory_space=pltpu.SMEM),
                      pl.BlockSpec((tokens * SUBLANES, LANES), lambda i, a, b: (i, 0))],
            out_specs=pl.BlockSpec(memory_space=pl.ANY),
            scratch_shapes=[pltpu.VMEM((SUBLANES, LANES), F32), pltpu.SemaphoreType.DMA((1,)),
                            pltpu.SemaphoreType.DMA((DMA_SLOTS,))]),
        out_shape=jax.ShapeDtypeStruct((n_rows * SUBLANES, LANES), F32),
        compiler_params=pltpu.CompilerParams(dimension_semantics=("arbitrary",), disable_bounds_checks=True),
        name="dispatch",
    )(zstart, zcount, dest_flat, xn2)


def _expert_kernel(be_ref, nu_ref, xs_ref, wgu_ref, bgu_ref, wd_ref, bd_ref, ys_ref, wgu_bf, wd_bf, *, rows):
    b = pl.program_id(0)

    @pl.when((b == 0) | (be_ref[b] != be_ref[jnp.maximum(b - 1, 0)]))
    def _():
        for r in range(0, D_MODEL, CAST_ROWS):
            wgu_bf[r:r + CAST_ROWS, :] = wgu_ref[r:r + CAST_ROWS, :].astype(BF16)
            wd_bf[r:r + CAST_ROWS, :] = wd_ref[r:r + CAST_ROWS, :].astype(BF16)

    @pl.when(b < nu_ref[0])
    def _():
        x = _row_tiles_load(xs_ref, 0, rows).astype(BF16)
        gu = _dot(x, wgu_bf[...]) + bgu_ref[...]
        glu = jnp.minimum(gu[:, :D_FF], SWIGLU_LIMIT)
        lin = jnp.clip(gu[:, D_FF:], -SWIGLU_LIMIT, SWIGLU_LIMIT)
        act = glu * (1.0 / (1.0 + jnp.exp(-SWIGLU_ALPHA * glu))) * (lin + 1.0)
        y = _dot(act.astype(BF16), wd_bf[...]) + bd_ref[...]
        _row_tiles_store(ys_ref, y, rows)

    @pl.when(b >= nu_ref[0])
    def _():
        ys_ref[...] = jnp.zeros_like(ys_ref)


def _experts(blk_expert, n_used, xs2, wgu, bgu, wd, bd):
    rows = MOE_BLOCK
    n_blocks = xs2.shape[0] // (rows * SUBLANES)
    by_expert = lambda shp: pl.BlockSpec((None,) + shp, lambda b, be, nu: (be[b], 0, 0))
    return pl.pallas_call(
        functools.partial(_expert_kernel, rows=rows),
        grid_spec=pltpu.PrefetchScalarGridSpec(
            num_scalar_prefetch=2, grid=(n_blocks,),
            in_specs=[pl.BlockSpec((rows * SUBLANES, LANES),
                                   lambda b, be, nu: (jnp.maximum(jnp.minimum(b, nu[0] - 1), 0), 0)),
                      by_expert((D_MODEL, 2 * D_FF)), by_expert((1, 2 * D_FF)),
                      by_expert((D_FF, D_MODEL)), by_expert((1, D_MODEL))],
            out_specs=pl.BlockSpec((rows * SUBLANES, LANES), lambda b, be, nu: (b, 0)),
            scratch_shapes=[pltpu.VMEM((D_MODEL, 2 * D_FF), BF16), pltpu.VMEM((D_FF, D_MODEL), BF16)]),
        out_shape=jax.ShapeDtypeStruct(xs2.shape, F32),
        compiler_params=pltpu.CompilerParams(dimension_semantics=("arbitrary",), vmem_limit_bytes=VMEM_LIMIT),
        name="experts",
    )(blk_expert, n_used, xs2, wgu, bgu, wd, bd)


def _combine_kernel(dest_ref, h_ref, gate_ref, fw_ref, ys_hbm, out_ref, ybuf, sem, *, tokens):
    def row_copy(j):
        return pltpu.make_async_copy(ys_hbm.at[_tile_rows(dest_ref[j]), :],
                                     ybuf.at[_tile_rows((j & (TOP_K - 1)) * tokens + (j >> 2)), :], sem.at[0])

    _start_row_burst(tokens * TOP_K, row_copy)
    pltpu.make_async_copy(ys_hbm.at[pl.ds(0, TOP_K * tokens * SUBLANES), :], ybuf, sem.at[0]).wait()

    acc = h_ref[...]
    gate = gate_ref[...]
    for k in range(TOP_K):
        acc = acc + gate[:, k:k + 1] * _row_tiles_load(ybuf, k * tokens, tokens)
    out_ref[...] = _rms(acc, fw_ref[...])


def _combine(dest_flat, h2, gates, fw, ys2):
    n_tok = h2.shape[0]
    tokens = DISPATCH_ROWS
    return pl.pallas_call(
        functools.partial(_combine_kernel, tokens=tokens),
        grid=(n_tok // tokens,),
        in_specs=[pl.BlockSpec((tokens * TOP_K,), lambda i: (i,), memory_space=pltpu.SMEM),
                  pl.BlockSpec((tokens, D_MODEL), lambda i: (i, 0)),
                  pl.BlockSpec((tokens, LANES), lambda i: (i, 0)),
                  pl.BlockSpec((1, D_MODEL), lambda i: (0, 0)),
                  pl.BlockSpec(memory_space=pl.ANY)],
        out_specs=pl.BlockSpec((tokens, D_MODEL), lambda i: (i, 0)),
        out_shape=jax.ShapeDtypeStruct((n_tok, D_MODEL), F32),
        scratch_shapes=[pltpu.VMEM((TOP_K * tokens * SUBLANES, LANES), F32),
                        pltpu.SemaphoreType.DMA((1,))],
        compiler_params=pltpu.CompilerParams(dimension_semantics=("arbitrary",), vmem_limit_bytes=VMEM_LIMIT,
                                             disable_bounds_checks=True),
        name="combine",
    )(dest_flat, h2, gates, fw, ys2)


def _moe(h2, norm_ffn_w, router_w, router_b, w_gate_up, b_gate_up, w_down, b_down, norm_final_w):
    n_tok = h2.shape[0]
    n_assign = n_tok * TOP_K
    n_blocks = -(-(n_assign + N_EXPERTS * (MOE_BLOCK - 1)) // MOE_BLOCK)
    n_rows = n_blocks * MOE_BLOCK

    rw = jnp.pad(router_w, ((0, 0), (0, LANES - N_EXPERTS)))
    rb = jnp.pad(router_b, (0, LANES - N_EXPERTS))[None, :]
    xn2, idx_t, rank_t, gates, cnt = _router(h2, norm_ffn_w[None, :], rw, rb)

    counts = cnt[0, :N_EXPERTS].astype(I32)
    padded = ((counts + MOE_BLOCK - 1) // MOE_BLOCK) * MOE_BLOCK
    pad_end = jnp.cumsum(padded)
    pad_start = pad_end - padded
    blk_first_row = jnp.arange(n_blocks, dtype=I32) * MOE_BLOCK
    blk_expert = jnp.minimum(jnp.sum((pad_end[None, :] <= blk_first_row[:, None]).astype(I32), axis=1),
                             N_EXPERTS - 1)
    n_used = (pad_end[-1:] // MOE_BLOCK).astype(I32)
    zstart = jnp.concatenate([pad_start + counts, pad_end[-1:]]).astype(I32)
    zcount = jnp.concatenate([padded - counts, n_rows - pad_end[-1:]]).astype(I32)

    dest_t = _dest(pad_start.astype(I32), idx_t, rank_t)
    dest_flat = dest_t[:TOP_K].T.reshape(-1)

    xs2 = _dispatch(zstart, zcount, dest_flat, xn2, n_rows)
    ys2 = _experts(blk_expert, n_used, xs2, w_gate_up, b_gate_up[:, None, :], w_down, b_down[:, None, :])
    return _combine(dest_flat, h2, gates, norm_final_w[None, :], ys2)


def kernel(x, meta_tokens, norm_mix_w, w_in, conv_short_w, conv_norm_w, ssd_conv_w, ssd_conv_b, ssd_dt_bias,
           ssd_a_log, ssd_d, ssd_norm_w, w_out, norm_ffn_w, router_w, router_b, w_gate_up, b_gate_up, w_down,
           b_down, norm_final_w):
    bsz, seq, _ = x.shape
    wts = _mixer_weights(norm_mix_w[0], w_in[0], conv_short_w[0], conv_norm_w[0], ssd_conv_w[0], ssd_conv_b[0],
                         ssd_dt_bias[0], ssd_a_log[0], ssd_d[0], ssd_norm_w[0], w_out[0])
    h = _run_mixer(x, meta_tokens, wts)
    out = _moe(h.reshape(bsz * seq, D_MODEL), norm_ffn_w[0], router_w[0], router_b[0], w_gate_up[0],
               b_gate_up[0], w_down[0], b_down[0], norm_final_w)
    return out.reshape(bsz, seq, D_MODEL)
```

```python
import functools

import jax
import jax.numpy as jnp
from jax import lax
from jax.experimental import pallas as pl
from jax.experimental.pallas import tpu as pltpu

F32 = jnp.float32
BF16 = jnp.bfloat16
I32 = jnp.int32

D_MODEL = 1024
N_META = 16
CONV_WIDTH = 3
CONV_HEADS = 8
SSD_HEADS = 16
SSD_HEAD_DIM = 64
D_SSD = SSD_HEADS * SSD_HEAD_DIM
SSD_GROUPS = 4
D_STATE = 128
SSD_CONV = 4
CHUNK = 128
N_EXPERTS = 32
TOP_K = 4
D_FF = 1024
SWIGLU_ALPHA = 1.702
SWIGLU_LIMIT = 7.0
EPS = 1e-5

LANES = 128
SUBLANES = 8
VMEM_LIMIT = 56 * 1024 * 1024

OFF_CB, OFF_CC, OFF_CX, OFF_Z, OFF_XBC, OFF_DT = 0, 1024, 2048, 3072, 4096, 6144
D_XBC = D_SSD + 2 * SSD_GROUPS * D_STATE
D_IN_PAD = OFF_DT + LANES
LANES_PER_ROW = D_MODEL // LANES

MIX_ROWS = 256
PROJ_COLS = 512
MIX_COLS = 512
ROUTE_ROWS = 256
MOE_BLOCK = 256
DISPATCH_ROWS = 512
DMA_SLOTS = 64
ISSUE_UNROLL = 8
CAST_ROWS = 128


def _rms(x, w):
    return x * lax.rsqrt(jnp.mean(x * x, axis=-1, keepdims=True) + EPS) * w


def _silu(x):
    return x * (1.0 / (1.0 + jnp.exp(-x)))


def _dot(a, b):
    return jnp.dot(a, b, preferred_element_type=F32)


def _split3(x):
    hi = x.astype(BF16)
    rest = x - hi.astype(F32)
    mid = rest.astype(BF16)
    lo = (rest - mid.astype(F32)).astype(BF16)
    return hi, mid, lo


def _mixer_project(x_ref, nw_ref, win_ref, proj, rows):
    n_bf = _rms(x_ref[...], nw_ref[...]).astype(BF16)

    def chunk(c0):
        c1 = min(c0 + PROJ_COLS, D_IN_PAD)

        def run():
            proj[SUBLANES:SUBLANES + rows, c0:c1] = _dot(n_bf, win_ref[:, c0:c1])

        return run

    return [chunk(c0) for c0 in range(0, D_IN_PAD, PROJ_COLS)]


def _mixer_mix(proj, side, x_ref, cw_ref, cnw_ref, xw_ref, xb_ref, dtb_ref, alog_ref, dvec_ref, snw_ref, wout_ref,
               h_ref, st_ref, tu_ref, tx_ref, state, tailu, tailx, ubuf, xc, ymix, *, rows, n_inert):
    side = list(side)

    def emit(n):
        for _ in range(n):
            if side:
                side.pop(0)()

    n_chunks = rows // CHUNK
    body = slice(SUBLANES, SUBLANES + rows)
    proj[0:SUBLANES, OFF_XBC:OFF_XBC + D_XBC] = tailx[...]
    ubuf[0:SUBLANES, :] = tailu[...]

    for c0 in range(0, D_MODEL, MIX_COLS):
        cs = slice(c0, c0 + MIX_COLS)
        ubuf[body, cs] = proj[body, OFF_CC + c0:OFF_CC + c0 + MIX_COLS] * proj[body, OFF_CX + c0:OFF_CX + c0 + MIX_COLS]
        conv = cw_ref[0:1, cs] * ubuf[SUBLANES - 2:SUBLANES - 2 + rows, cs]
        conv = conv + cw_ref[1:2, cs] * ubuf[SUBLANES - 1:SUBLANES - 1 + rows, cs]
        conv = conv + cw_ref[2:3, cs] * ubuf[SUBLANES:SUBLANES + rows, cs]
        yc = proj[body, OFF_CB + c0:OFF_CB + c0 + MIX_COLS] * conv
        for g in range(MIX_COLS // LANES):
            sl = slice(c0 + g * LANES, c0 + (g + 1) * LANES)
            ymix[:, sl] = _rms(yc[:, g * LANES:(g + 1) * LANES], cnw_ref[:, sl]).astype(BF16)
        tailu[:, cs] = ubuf[rows:rows + SUBLANES, cs]
        emit(1)
    h_ref[...] = x_ref[...] + _dot(ymix[:, 0:D_MODEL], wout_ref[0:D_MODEL, :])

    for c0 in range(0, D_XBC, MIX_COLS):
        cs = slice(c0, c0 + MIX_COLS)
        ps = slice(OFF_XBC + c0, OFF_XBC + c0 + MIX_COLS)
        acc = xb_ref[:, cs] + xw_ref[0:1, cs] * proj[SUBLANES - 3:SUBLANES - 3 + rows, ps]
        for k in range(1, SSD_CONV):
            acc = acc + xw_ref[k:k + 1, cs] * proj[SUBLANES - 3 + k:SUBLANES - 3 + k + rows, ps]
        xc[:, cs] = _silu(acc)
        tailx[:, cs] = proj[rows:rows + SUBLANES, ps]
        emit(1)

    dt_raw = proj[body, OFF_DT:OFF_DT + LANES] + dtb_ref[...]
    dt = jnp.maximum(dt_raw, 0.0) + jnp.log(1.0 + jnp.exp(-jnp.abs(dt_raw)))
    if n_inert:
        r_all = lax.broadcasted_iota(I32, (rows, LANES), 0)
        dt = jnp.where(r_all >= n_inert, dt, 0.0)
    a_neg = -jnp.exp(alog_ref[...])
    da = dt * a_neg

    row = lax.broadcasted_iota(I32, (CHUNK, CHUNK), 0)
    lane = lax.broadcasted_iota(I32, (CHUNK, CHUNK), 1)
    causal = row >= lane
    tri = causal.astype(BF16)
    left = lane < SSD_HEAD_DIM

    def expand_pair(v, p):
        return jnp.where(left, v[:, 2 * p:2 * p + 1], v[:, 2 * p + 1:2 * p + 2])

    for c in range(n_chunks):
        r0 = c * CHUNK
        rs = slice(r0, r0 + CHUNK)
        dt_c = dt[rs, :]
        da_hi, da_mid, da_lo = _split3(da[rs, :])
        a_cs = _dot(tri, da_hi) + _dot(tri, da_mid) + _dot(tri, da_lo)
        a_cs_t = a_cs.T
        a_last = a_cs[CHUNK - 1:CHUNK, :]
        e_cs = jnp.exp(a_cs)
        e_end = jnp.exp(a_last - a_cs)
        e_last = jnp.exp(a_last)
        for g in range(SSD_GROUPS):
            b_g = xc[rs, D_SSD + g * D_STATE:D_SSD + (g + 1) * D_STATE]
            c_g = xc[rs, D_SSD + SSD_GROUPS * D_STATE + g * D_STATE:
                     D_SSD + SSD_GROUPS * D_STATE + (g + 1) * D_STATE].astype(BF16)
            b_gt = b_g.T.astype(BF16)
            cb = _dot(c_g, b_gt)
            gsl = slice(g * 2 * LANES, (g + 1) * 2 * LANES)
            y_off = _dot(c_g, state[:, gsl].astype(BF16))
            xdt_e = []
            for q in range(2):
                p = 2 * g + q
                psl = slice(p * LANES, (p + 1) * LANES)
                xs_p = xc[rs, psl]
                xdt_p = xs_p * expand_pair(dt_c, p)
                m = []
                for hh in range(2):
                    hd = 2 * p + hh
                    seg = a_cs[:, hd:hd + 1] - a_cs_t[hd:hd + 1, :]
                    dec = jnp.where(causal, jnp.exp(jnp.where(causal, seg, 0.0)), 0.0)
                    m.append((cb * dec).astype(BF16))
                lhs = jnp.concatenate(m, axis=1)
                rhs = jnp.concatenate([jnp.where(left, xdt_p, 0.0), jnp.where(left, 0.0, xdt_p)],
                                      axis=0).astype(BF16)
                y_p = _dot(lhs, rhs) + y_off[:, q * LANES:(q + 1) * LANES] * expand_pair(e_cs, p)
                y_p = y_p + dvec_ref[:, psl] * xs_p
                xdt_e.append((xdt_p * expand_pair(e_end, p)).astype(BF16))
                z_p = proj[SUBLANES + r0:SUBLANES + r0 + CHUNK, OFF_Z + p * LANES:OFF_Z + (p + 1) * LANES]
                xc[rs, psl] = y_p * _silu(z_p)
            dec_row = jnp.concatenate([expand_pair(e_last, 2 * g), expand_pair(e_last, 2 * g + 1)], axis=1)
            upd = _dot(b_gt, jnp.concatenate(xdt_e, axis=1))
            state[:, gsl] = dec_row * state[:, gsl] + upd
            ymix[rs, D_MODEL + g * 2 * LANES:D_MODEL + (g + 1) * 2 * LANES] = _rms(
                xc[rs, gsl], snw_ref[:, gsl]).astype(BF16)
            emit(1)

    emit(len(side))
    h_ref[...] += _dot(ymix[:, D_MODEL:2 * D_MODEL], wout_ref[D_MODEL:2 * D_MODEL, :])
    st_ref[...] = state[...]
    tu_ref[...] = tailu[...]
    tx_ref[...] = tailx[...]


def _mixer_kernel(xa_ref, xb_ref, st0_ref, tu0_ref, tx0_ref, nw_ref, win_ref, cw_ref, cnw_ref, xw_ref, xbias_ref,
                  dtb_ref, alog_ref, dvec_ref, snw_ref, wout_ref,
                  h_ref, st_ref, tu_ref, tx_ref,
                  proj_even, proj_odd, state, tailu, tailx, ubuf, xc, ymix, *, rows, n_inert, tiles_per_seq):
    s = pl.program_id(0)

    @pl.when(s == 0)
    def _():
        proj_odd[...] = jnp.zeros_like(proj_odd)

    @pl.when(jnp.maximum(s - 1, 0) % tiles_per_seq == 0)
    def _():
        state[...] = st0_ref[...]
        tailu[...] = tu0_ref[...]
        tailx[...] = tx0_ref[...]

    mix = functools.partial(
        _mixer_mix, x_ref=xb_ref, cw_ref=cw_ref, cnw_ref=cnw_ref, xw_ref=xw_ref, xb_ref=xbias_ref, dtb_ref=dtb_ref,
        alog_ref=alog_ref, dvec_ref=dvec_ref, snw_ref=snw_ref, wout_ref=wout_ref, h_ref=h_ref, st_ref=st_ref,
        tu_ref=tu_ref, tx_ref=tx_ref, state=state, tailu=tailu, tailx=tailx, ubuf=ubuf, xc=xc, ymix=ymix,
        rows=rows, n_inert=n_inert)

    @pl.when(s % 2 == 0)
    def _():
        mix(proj_odd, _mixer_project(xa_ref, nw_ref, win_ref, proj_even, rows))

    @pl.when(s % 2 == 1)
    def _():
        mix(proj_even, _mixer_project(xa_ref, nw_ref, win_ref, proj_odd, rows))


def _mixer(x, st0, tu0, tx0, wts, *, rows, n_inert):
    bsz, length, _ = x.shape
    tiles_per_seq = length // rows
    n_tiles = bsz * tiles_per_seq
    const = lambda shp: pl.BlockSpec(shp, lambda s: (0,) * len(shp))
    resident = lambda shp: pl.BlockSpec(shp, lambda s: (0,) * len(shp), pipeline_mode=pl.Buffered(1))
    projected = lambda s: jnp.minimum(s, n_tiles - 1)
    mixed = lambda s: jnp.maximum(s - 1, 0)
    (nw, win, cw, cnw, xw, xb, dtb, alog, dvec, snw, wout) = wts
    kern = functools.partial(_mixer_kernel, rows=rows, n_inert=n_inert, tiles_per_seq=tiles_per_seq)
    return pl.pallas_call(
        kern,
        grid=(n_tiles + 1,),
        in_specs=[
            pl.BlockSpec((None, rows, D_MODEL),
                         lambda s: (projected(s) // tiles_per_seq, projected(s) % tiles_per_seq, 0)),
            pl.BlockSpec((None, rows, D_MODEL), lambda s: (mixed(s) // tiles_per_seq, mixed(s) % tiles_per_seq, 0)),
            const((D_STATE, D_SSD)), const((SUBLANES, D_MODEL)), const((SUBLANES, D_XBC)),
            const((1, D_MODEL)), resident((D_MODEL, D_IN_PAD)), const((SUBLANES, D_MODEL)), const((1, D_MODEL)),
            const((SUBLANES, D_XBC)), const((1, D_XBC)), const((1, LANES)), const((1, LANES)),
            const((1, D_SSD)), const((1, D_SSD)), resident((2 * D_MODEL, D_MODEL)),
        ],
        out_specs=[
            pl.BlockSpec((None, rows, D_MODEL), lambda s: (mixed(s) // tiles_per_seq, mixed(s) % tiles_per_seq, 0)),
            pl.BlockSpec((None, D_STATE, D_SSD), lambda s: (mixed(s) // tiles_per_seq, 0, 0)),
            pl.BlockSpec((None, SUBLANES, D_MODEL), lambda s: (mixed(s) // tiles_per_seq, 0, 0)),
            pl.BlockSpec((None, SUBLANES, D_XBC), lambda s: (mixed(s) // tiles_per_seq, 0, 0)),
        ],
        out_shape=[
            jax.ShapeDtypeStruct((bsz, length, D_MODEL), F32),
            jax.ShapeDtypeStruct((bsz, D_STATE, D_SSD), F32),
            jax.ShapeDtypeStruct((bsz, SUBLANES, D_MODEL), F32),
            jax.ShapeDtypeStruct((bsz, SUBLANES, D_XBC), F32),
        ],
        scratch_shapes=[
            pltpu.VMEM((rows + SUBLANES, D_IN_PAD), F32),
            pltpu.VMEM((rows + SUBLANES, D_IN_PAD), F32),
            pltpu.VMEM((D_STATE, D_SSD), F32),
            pltpu.VMEM((SUBLANES, D_MODEL), F32),
            pltpu.VMEM((SUBLANES, D_XBC), F32),
            pltpu.VMEM((rows + SUBLANES, D_MODEL), F32),
            pltpu.VMEM((rows, D_XBC), F32),
            pltpu.VMEM((rows, 2 * D_MODEL), BF16),
        ],
        compiler_params=pltpu.CompilerParams(dimension_semantics=("arbitrary",), vmem_limit_bytes=VMEM_LIMIT),
        name="mixer",
    )(x, x, st0, tu0, tx0, nw, win, cw, cnw, xw, xb, dtb, alog, dvec, snw, wout)


def _mixer_weights(norm_mix_w, w_in, conv_short_w, conv_norm_w, ssd_conv_w, ssd_conv_b, ssd_dt_bias,
                   ssd_a_log, ssd_d, ssd_norm_w, w_out):
    pad_rows = lambda w: jnp.pad(w, ((0, SUBLANES - w.shape[0]), (0, 0)))
    pad_lanes = lambda v: jnp.pad(v, (0, LANES - v.shape[0]))[None, :]
    win = jnp.pad(w_in, ((0, 0), (0, D_IN_PAD - w_in.shape[1]))).astype(BF16)
    return (norm_mix_w[None, :], win, pad_rows(conv_short_w), conv_norm_w[None, :], pad_rows(ssd_conv_w),
            ssd_conv_b[None, :], pad_lanes(ssd_dt_bias), pad_lanes(ssd_a_log),
            jnp.repeat(ssd_d, SSD_HEAD_DIM)[None, :], ssd_norm_w[None, :], w_out.astype(BF16))


def _run_mixer(x, meta_tokens, wts):
    zeros = lambda *s: jnp.zeros(s, F32)
    meta = jnp.concatenate([zeros(CHUNK - N_META, D_MODEL), meta_tokens.astype(F32)], axis=0)[None]
    _, st0, tu0, tx0 = _mixer(meta, zeros(D_STATE, D_SSD), zeros(SUBLANES, D_MODEL), zeros(SUBLANES, D_XBC),
                              wts, rows=CHUNK, n_inert=CHUNK - N_META)
    h, _, _, _ = _mixer(x, st0[0], tu0[0], tx0[0], wts, rows=MIX_ROWS, n_inert=0)
    return h


def _row_tiles_store(ref, val, rows):
    for j in range(LANES_PER_ROW):
        ref[pl.ds(j, rows, stride=SUBLANES), :] = val[:, j * LANES:(j + 1) * LANES]


def _row_tiles_load(ref, first_row, rows):
    return jnp.concatenate(
        [ref[pl.ds(first_row * SUBLANES + j, rows, stride=SUBLANES), :] for j in range(LANES_PER_ROW)], axis=1)


def _router_kernel(h_ref, nw_ref, rw_ref, rb_ref, xn_ref, idx_ref, rank_ref, gate_ref, cnt_ref, carry, *, rows):
    @pl.when(pl.program_id(0) == 0)
    def _():
        carry[...] = jnp.zeros_like(carry)

    xn = _rms(h_ref[...], nw_ref[...])
    _row_tiles_store(xn_ref, xn, rows)

    lane = lax.broadcasted_iota(I32, (rows, LANES), 1).astype(F32)
    x_hi = xn.astype(BF16)
    x_lo = (xn - x_hi.astype(F32)).astype(BF16)
    w = rw_ref[...]
    w_hi = w.astype(BF16)
    w_lo = (w - w_hi.astype(F32)).astype(BF16)
    logits = _dot(x_hi, w_hi) + _dot(x_lo, w_hi) + _dot(x_hi, w_lo) + rb_ref[...]
    cur = jnp.where(lane < N_EXPERTS, logits, -jnp.inf)
    idxs, vals = [], []
    for _ in range(TOP_K):
        m = jnp.max(cur, axis=1, keepdims=True)
        ik = jnp.min(jnp.where(cur == m, lane, float(LANES)), axis=1, keepdims=True)
        cur = jnp.where(lane == ik, -jnp.inf, cur)
        idxs.append(ik)
        vals.append(m)
    es = [jnp.exp(v - vals[0]) for v in vals]
    den = es[0] + es[1] + es[2] + es[3]
    gates = [e / den for e in es]

    onehot = jnp.zeros((rows, LANES), F32)
    for ik in idxs:
        onehot = onehot + (lane == ik).astype(F32)
    r_i = lax.broadcasted_iota(I32, (rows, rows), 0)
    c_i = lax.broadcasted_iota(I32, (rows, rows), 1)
    before = (r_i > c_i).astype(BF16)
    rankmat = _dot(before, onehot.astype(BF16)) + carry[...]
    carry[...] = carry[...] + jnp.sum(onehot, axis=0, keepdims=True)
    ranks = [jnp.sum(jnp.where(lane == ik, rankmat, 0.0), axis=1, keepdims=True) for ik in idxs]

    def pack(cols):
        m = jnp.zeros((rows, LANES), F32)
        for k, col in enumerate(cols):
            m = jnp.where(lane == k, col, m)
        return m

    idx_ref[...] = pack(idxs).T[0:SUBLANES, :].astype(I32)
    rank_ref[...] = pack(ranks).T[0:SUBLANES, :].astype(I32)
    gate_ref[...] = pack(gates)
    cnt_ref[...] = jnp.broadcast_to(carry[...], (SUBLANES, LANES))


def _router(h2, nw, rw, rb):
    n_tok = h2.shape[0]
    rows = ROUTE_ROWS
    const = lambda shp: pl.BlockSpec(shp, lambda i: (0,) * len(shp))
    return pl.pallas_call(
        functools.partial(_router_kernel, rows=rows),
        grid=(n_tok // rows,),
        in_specs=[pl.BlockSpec((rows, D_MODEL), lambda i: (i, 0)), const((1, D_MODEL)),
                  const((D_MODEL, LANES)), const((1, LANES))],
        out_specs=[
            pl.BlockSpec((rows * SUBLANES, LANES), lambda i: (i, 0)),
            pl.BlockSpec((SUBLANES, rows), lambda i: (0, i)),
            pl.BlockSpec((SUBLANES, rows), lambda i: (0, i)),
            pl.BlockSpec((rows, LANES), lambda i: (i, 0)),
            const((SUBLANES, LANES)),
        ],
        out_shape=[
            jax.ShapeDtypeStruct((n_tok * SUBLANES, LANES), F32),
            jax.ShapeDtypeStruct((SUBLANES, n_tok), I32),
            jax.ShapeDtypeStruct((SUBLANES, n_tok), I32),
            jax.ShapeDtypeStruct((n_tok, LANES), F32),
            jax.ShapeDtypeStruct((SUBLANES, LANES), F32),
        ],
        scratch_shapes=[pltpu.VMEM((1, LANES), F32)],
        compiler_params=pltpu.CompilerParams(dimension_semantics=("arbitrary",), vmem_limit_bytes=VMEM_LIMIT),
        name="router",
    )(h2, nw, rw, rb)


def _dest_kernel(start_ref, idx_ref, rank_ref, dest_ref):
    idx = idx_ref[...]
    dest = rank_ref[...]
    for e in range(N_EXPERTS):
        dest = dest + jnp.where(idx == e, start_ref[e], 0)
    dest_ref[...] = dest


def _dest(pad_start, idx_t, rank_t):
    n_tok = idx_t.shape[1]
    cols = min(n_tok, 4096)
    spec = pl.BlockSpec((SUBLANES, cols), lambda i, s: (0, i))
    return pl.pallas_call(
        _dest_kernel,
        grid_spec=pltpu.PrefetchScalarGridSpec(
            num_scalar_prefetch=1, grid=(n_tok // cols,), in_specs=[spec, spec], out_specs=spec),
        out_shape=jax.ShapeDtypeStruct((SUBLANES, n_tok), I32),
        name="dest",
    )(pad_start, idx_t, rank_t)


def _pipelined_row_copies(n, make_copy):
    def start_only(j, c):
        make_copy(j).start()
        return c

    def wait_then_start(j, c):
        make_copy(j - DMA_SLOTS).wait()
        make_copy(j).start()
        return c

    def wait_only(j, c):
        make_copy(j).wait()
        return c

    head = jnp.minimum(n, DMA_SLOTS)
    lax.fori_loop(0, head, start_only, 0)
    lax.fori_loop(head, n, wait_then_start, 0)
    lax.fori_loop(jnp.maximum(n - DMA_SLOTS, 0), n, wait_only, 0)


def _start_row_burst(n_rows, make_copy):
    def body(g, c):
        for u in range(ISSUE_UNROLL):
            make_copy(g * ISSUE_UNROLL + u).start(priority=u % 2)
        return c

    lax.fori_loop(0, n_rows // ISSUE_UNROLL, body, 0)


def _tile_rows(row):
    return pl.ds(pl.multiple_of(row * SUBLANES, SUBLANES), SUBLANES)


def _dispatch_kernel(zstart_ref, zcount_ref, dest_ref, xn_ref, xs_hbm, zero_row, sem, zsem, *, tokens):
    i = pl.program_id(0)

    def token_copy(j):
        return pltpu.make_async_copy(xn_ref.at[_tile_rows(j >> 2), :], xs_hbm.at[_tile_rows(dest_ref[j]), :],
                                     sem.at[0])

    _start_row_burst(tokens * TOP_K, token_copy)
    for _ in range(TOP_K):
        pltpu.make_async_copy(xn_ref, xs_hbm.at[pl.ds(0, tokens * SUBLANES), :], sem.at[0]).wait()

    @pl.when(i == pl.num_programs(0) - 1)
    def _():
        zero_row[...] = jnp.zeros_like(zero_row)

        def fill_segment(s, c):
            first = zstart_ref[s]

            def zero_copy(j):
                return pltpu.make_async_copy(zero_row, xs_hbm.at[_tile_rows(first + j), :],
                                             zsem.at[j & (DMA_SLOTS - 1)])

            _pipelined_row_copies(zcount_ref[s], zero_copy)
            return c

        lax.fori_loop(0, N_EXPERTS + 1, fill_segment, 0)


def _dispatch(zstart, zcount, dest_flat, xn2, n_rows):
    n_tok = xn2.shape[0] // SUBLANES
    tokens = DISPATCH_ROWS
    return pl.pallas_call(
        functools.partial(_dispatch_kernel, tokens=tokens),
        grid_spec=pltpu.PrefetchScalarGridSpec(
            num_scalar_prefetch=2, grid=(n_tok // tokens,),
            in_specs=[pl.BlockSpec((tokens * TOP_K,), lambda i, a, b: (i,), memory_space=pltpu.SMEM),
                      pl.BlockSpec((tokens * SUBLANES, LANES), lambda i, a, b: (i, 0))],
            out_specs=pl.BlockSpec(memory_space=pl.ANY),
            scratch_shapes=[pltpu.VMEM((SUBLANES, LANES), F32), pltpu.SemaphoreType.DMA((1,)),
                            pltpu.SemaphoreType.DMA((DMA_SLOTS,))]),
        out_shape=jax.ShapeDtypeStruct((n_rows * SUBLANES, LANES), F32),
        compiler_params=pltpu.CompilerParams(dimension_semantics=("arbitrary",), disable_bounds_checks=True),
        name="dispatch",
    )(zstart, zcount, dest_flat, xn2)


def _expert_kernel(be_ref, nu_ref, xs_ref, wgu_ref, bgu_ref, wd_ref, bd_ref, ys_ref, wgu_bf, wd_bf, *, rows):
    b = pl.program_id(0)

    @pl.when((b == 0) | (be_ref[b] != be_ref[jnp.maximum(b - 1, 0)]))
    def _():
        for r in range(0, D_MODEL, CAST_ROWS):
            wgu_bf[r:r + CAST_ROWS, :] = wgu_ref[r:r + CAST_ROWS, :].astype(BF16)
            wd_bf[r:r + CAST_ROWS, :] = wd_ref[r:r + CAST_ROWS, :].astype(BF16)

    @pl.when(b < nu_ref[0])
    def _():
        x = _row_tiles_load(xs_ref, 0, rows).astype(BF16)
        gu = _dot(x, wgu_bf[...]) + bgu_ref[...]
        glu = jnp.minimum(gu[:, :D_FF], SWIGLU_LIMIT)
        lin = jnp.clip(gu[:, D_FF:], -SWIGLU_LIMIT, SWIGLU_LIMIT)
        act = glu * (1.0 / (1.0 + jnp.exp(-SWIGLU_ALPHA * glu))) * (lin + 1.0)
        y = _dot(act.astype(BF16), wd_bf[...]) + bd_ref[...]
        _row_tiles_store(ys_ref, y, rows)

    @pl.when(b >= nu_ref[0])
    def _():
        ys_ref[...] = jnp.zeros_like(ys_ref)


def _experts(blk_expert, n_used, xs2, wgu, bgu, wd, bd):
    rows = MOE_BLOCK
    n_blocks = xs2.shape[0] // (rows * SUBLANES)
    by_expert = lambda shp: pl.BlockSpec((None,) + shp, lambda b, be, nu: (be[b], 0, 0))
    return pl.pallas_call(
        functools.partial(_expert_kernel, rows=rows),
        grid_spec=pltpu.PrefetchScalarGridSpec(
            num_scalar_prefetch=2, grid=(n_blocks,),
            in_specs=[pl.BlockSpec((rows * SUBLANES, LANES),
                                   lambda b, be, nu: (jnp.maximum(jnp.minimum(b, nu[0] - 1), 0), 0)),
                      by_expert((D_MODEL, 2 * D_FF)), by_expert((1, 2 * D_FF)),
                      by_expert((D_FF, D_MODEL)), by_expert((1, D_MODEL))],
            out_specs=pl.BlockSpec((rows * SUBLANES, LANES), lambda b, be, nu: (b, 0)),
            scratch_shapes=[pltpu.VMEM((D_MODEL, 2 * D_FF), BF16), pltpu.VMEM((D_FF, D_MODEL), BF16)]),
        out_shape=jax.ShapeDtypeStruct(xs2.shape, F32),
        compiler_params=pltpu.CompilerParams(dimension_semantics=("arbitrary",), vmem_limit_bytes=VMEM_LIMIT),
        name="experts",
    )(blk_expert, n_used, xs2, wgu, bgu, wd, bd)


def _combine_kernel(dest_ref, h_ref, gate_ref, fw_ref, ys_hbm, out_ref, ybuf, sem, *, tokens):
    def row_copy(j):
        return pltpu.make_async_copy(ys_hbm.at[_tile_rows(dest_ref[j]), :],
                                     ybuf.at[_tile_rows((j & (TOP_K - 1)) * tokens + (j >> 2)), :], sem.at[0])

    _start_row_burst(tokens * TOP_K, row_copy)
    pltpu.make_async_copy(ys_hbm.at[pl.ds(0, TOP_K * tokens * SUBLANES), :], ybuf, sem.at[0]).wait()

    acc = h_ref[...]
    gate = gate_ref[...]
    for k in range(TOP_K):
        acc = acc + gate[:, k:k + 1] * _row_tiles_load(ybuf, k * tokens, tokens)
    out_ref[...] = _rms(acc, fw_ref[...])


def _combine(dest_flat, h2, gates, fw, ys2):
    n_tok = h2.shape[0]
    tokens = DISPATCH_ROWS
    return pl.pallas_call(
        functools.partial(_combine_kernel, tokens=tokens),
        grid=(n_tok // tokens,),
        in_specs=[pl.BlockSpec((tokens * TOP_K,), lambda i: (i,), memory_space=pltpu.SMEM),
                  pl.BlockSpec((tokens, D_MODEL), lambda i: (i, 0)),
                  pl.BlockSpec((tokens, LANES), lambda i: (i, 0)),
                  pl.BlockSpec((1, D_MODEL), lambda i: (0, 0)),
                  pl.BlockSpec(memory_space=pl.ANY)],
        out_specs=pl.BlockSpec((tokens, D_MODEL), lambda i: (i, 0)),
        out_shape=jax.ShapeDtypeStruct((n_tok, D_MODEL), F32),
        scratch_shapes=[pltpu.VMEM((TOP_K * tokens * SUBLANES, LANES), F32),
                        pltpu.SemaphoreType.DMA((1,))],
        compiler_params=pltpu.CompilerParams(dimension_semantics=("arbitrary",), vmem_limit_bytes=VMEM_LIMIT,
                                             disable_bounds_checks=True),
        name="combine",
    )(dest_flat, h2, gates, fw, ys2)


def _moe(h2, norm_ffn_w, router_w, router_b, w_gate_up, b_gate_up, w_down, b_down, norm_final_w):
    n_tok = h2.shape[0]
    n_assign = n_tok * TOP_K
    n_blocks = -(-(n_assign + N_EXPERTS * (MOE_BLOCK - 1)) // MOE_BLOCK)
    n_rows = n_blocks * MOE_BLOCK

    rw = jnp.pad(router_w, ((0, 0), (0, LANES - N_EXPERTS)))
    rb = jnp.pad(router_b, (0, LANES - N_EXPERTS))[None, :]
    xn2, idx_t, rank_t, gates, cnt = _router(h2, norm_ffn_w[None, :], rw, rb)

    counts = cnt[0, :N_EXPERTS].astype(I32)
    padded = ((counts + MOE_BLOCK - 1) // MOE_BLOCK) * MOE_BLOCK
    pad_end = jnp.cumsum(padded)
    pad_start = pad_end - padded
    blk_first_row = jnp.arange(n_blocks, dtype=I32) * MOE_BLOCK
    blk_expert = jnp.minimum(jnp.sum((pad_end[None, :] <= blk_first_row[:, None]).astype(I32), axis=1),
                             N_EXPERTS - 1)
    n_used = (pad_end[-1:] // MOE_BLOCK).astype(I32)
    zstart = jnp.concatenate([pad_start + counts, pad_end[-1:]]).astype(I32)
    zcount = jnp.concatenate([padded - counts, n_rows - pad_end[-1:]]).astype(I32)

    dest_t = _dest(pad_start.astype(I32), idx_t, rank_t)
    dest_flat = dest_t[:TOP_K].T.reshape(-1)

    xs2 = _dispatch(zstart, zcount, dest_flat, xn2, n_rows)
    ys2 = _experts(blk_expert, n_used, xs2, w_gate_up, b_gate_up[:, None, :], w_down, b_down[:, None, :])
    return _combine(dest_flat, h2, gates, norm_final_w[None, :], ys2)


def kernel(x, meta_tokens, norm_mix_w, w_in, conv_short_w, conv_norm_w, ssd_conv_w, ssd_conv_b, ssd_dt_bias,
           ssd_a_log, ssd_d, ssd_norm_w, w_out, norm_ffn_w, router_w, router_b, w_gate_up, b_gate_up, w_down,
           b_down, norm_final_w):
    bsz, seq, _ = x.shape
    wts = _mixer_weights(norm_mix_w[0], w_in[0], conv_short_w[0], conv_norm_w[0], ssd_conv_w[0], ssd_conv_b[0],
                         ssd_dt_bias[0], ssd_a_log[0], ssd_d[0], ssd_norm_w[0], w_out[0])
    h = _run_mixer(x, meta_tokens, wts)
    out = _moe(h.reshape(bsz * seq, D_MODEL), norm_ffn_w[0], router_w[0], router_b[0], w_gate_up[0],
               b_gate_up[0], w_down[0], b_down[0], norm_final_w)
    return out.reshape(bsz, seq, D_MODEL)
```

```python
import functools

import jax
import jax.numpy as jnp
from jax import lax
from jax.experimental import pallas as pl
from jax.experimental.pallas import tpu as pltpu

F32 = jnp.float32
BF16 = jnp.bfloat16
I32 = jnp.int32

D_MODEL = 1024
N_META = 16
CONV_WIDTH = 3
CONV_HEADS = 8
SSD_HEADS = 16
SSD_HEAD_DIM = 64
D_SSD = SSD_HEADS * SSD_HEAD_DIM
SSD_GROUPS = 4
D_STATE = 128
SSD_CONV = 4
CHUNK = 128
N_EXPERTS = 32
TOP_K = 4
D_FF = 1024
SWIGLU_ALPHA = 1.702
SWIGLU_LIMIT = 7.0
EPS = 1e-5

LANES = 128
SUBLANES = 8
VMEM_LIMIT = 56 * 1024 * 1024

OFF_CB, OFF_CC, OFF_CX, OFF_Z, OFF_XBC, OFF_DT = 0, 1024, 2048, 3072, 4096, 6144
D_XBC = D_SSD + 2 * SSD_GROUPS * D_STATE
D_IN_PAD = OFF_DT + LANES
LANES_PER_ROW = D_MODEL // LANES

MIX_ROWS = 256
PROJ_COLS = 512
MIX_COLS = 512
ROUTE_ROWS = 256
MOE_BLOCK = 256
FF_COLS = 512
COMBINE_ROWS = 512
SLOT_CHUNK = 4096
ISSUE_UNROLL = 8
CAST_ROWS = 128


def _rms(x, w):
    return x * lax.rsqrt(jnp.mean(x * x, axis=-1, keepdims=True) + EPS) * w


def _silu(x):
    return x * (1.0 / (1.0 + jnp.exp(-x)))


def _dot(a, b):
    return jnp.dot(a, b, preferred_element_type=F32)


def _split3(x):
    hi = x.astype(BF16)
    rest = x - hi.astype(F32)
    mid = rest.astype(BF16)
    lo = (rest - mid.astype(F32)).astype(BF16)
    return hi, mid, lo


def _mixer_project(x_ref, nw_ref, win_ref, proj, rows):
    n_bf = _rms(x_ref[...], nw_ref[...]).astype(BF16)

    def chunk(c0):
        c1 = min(c0 + PROJ_COLS, D_IN_PAD)

        def run():
            proj[SUBLANES:SUBLANES + rows, c0:c1] = _dot(n_bf, win_ref[:, c0:c1])

        return run

    return [chunk(c0) for c0 in range(0, D_IN_PAD, PROJ_COLS)]


def _mixer_mix(proj, side, x_ref, cw_ref, cnw_ref, xw_ref, xb_ref, dtb_ref, alog_ref, dvec_ref, snw_ref, wout_ref,
               h_ref, st_ref, tu_ref, tx_ref, state, tailu, tailx, ubuf, xc, ymix, *, rows, n_inert):
    side = list(side)

    def emit(n):
        for _ in range(n):
            if side:
                side.pop(0)()

    n_chunks = rows // CHUNK
    body = slice(SUBLANES, SUBLANES + rows)
    proj[0:SUBLANES, OFF_XBC:OFF_XBC + D_XBC] = tailx[...]
    ubuf[0:SUBLANES, :] = tailu[...]

    for c0 in range(0, D_MODEL, MIX_COLS):
        cs = slice(c0, c0 + MIX_COLS)
        ubuf[body, cs] = proj[body, OFF_CC + c0:OFF_CC + c0 + MIX_COLS] * proj[body, OFF_CX + c0:OFF_CX + c0 + MIX_COLS]
        conv = cw_ref[0:1, cs] * ubuf[SUBLANES - 2:SUBLANES - 2 + rows, cs]
        conv = conv + cw_ref[1:2, cs] * ubuf[SUBLANES - 1:SUBLANES - 1 + rows, cs]
        conv = conv + cw_ref[2:3, cs] * ubuf[SUBLANES:SUBLANES + rows, cs]
        yc = proj[body, OFF_CB + c0:OFF_CB + c0 + MIX_COLS] * conv
        for g in range(MIX_COLS // LANES):
            sl = slice(c0 + g * LANES, c0 + (g + 1) * LANES)
            ymix[:, sl] = _rms(yc[:, g * LANES:(g + 1) * LANES], cnw_ref[:, sl]).astype(BF16)
        tailu[:, cs] = ubuf[rows:rows + SUBLANES, cs]
        emit(1)
    h_ref[...] = x_ref[...] + _dot(ymix[:, 0:D_MODEL], wout_ref[0:D_MODEL, :])

    for c0 in range(0, D_XBC, MIX_COLS):
        cs = slice(c0, c0 + MIX_COLS)
        ps = slice(OFF_XBC + c0, OFF_XBC + c0 + MIX_COLS)
        acc = xb_ref[:, cs] + xw_ref[0:1, cs] * proj[SUBLANES - 3:SUBLANES - 3 + rows, ps]
        for k in range(1, SSD_CONV):
            acc = acc + xw_ref[k:k + 1, cs] * proj[SUBLANES - 3 + k:SUBLANES - 3 + k + rows, ps]
        xc[:, cs] = _silu(acc)
        tailx[:, cs] = proj[rows:rows + SUBLANES, ps]
        emit(1)

    dt_raw = proj[body, OFF_DT:OFF_DT + LANES] + dtb_ref[...]
    dt = jnp.maximum(dt_raw, 0.0) + jnp.log(1.0 + jnp.exp(-jnp.abs(dt_raw)))
    if n_inert:
        r_all = lax.broadcasted_iota(I32, (rows, LANES), 0)
        dt = jnp.where(r_all >= n_inert, dt, 0.0)
    a_neg = -jnp.exp(alog_ref[...])
    da = dt * a_neg

    row = lax.broadcasted_iota(I32, (CHUNK, CHUNK), 0)
    lane = lax.broadcasted_iota(I32, (CHUNK, CHUNK), 1)
    causal = row >= lane
    tri = causal.astype(BF16)
    left = lane < SSD_HEAD_DIM

    def expand_pair(v, p):
        return jnp.where(left, v[:, 2 * p:2 * p + 1], v[:, 2 * p + 1:2 * p + 2])

    for c in range(n_chunks):
        r0 = c * CHUNK
        rs = slice(r0, r0 + CHUNK)
        dt_c = dt[rs, :]
        da_hi, da_mid, da_lo = _split3(da[rs, :])
        a_cs = _dot(tri, da_hi) + _dot(tri, da_mid) + _dot(tri, da_lo)
        a_cs_t = a_cs.T
        a_last = a_cs[CHUNK - 1:CHUNK, :]
        e_cs = jnp.exp(a_cs)
        e_end = jnp.exp(a_last - a_cs)
        e_last = jnp.exp(a_last)
        for g in range(SSD_GROUPS):
            b_g = xc[rs, D_SSD + g * D_STATE:D_SSD + (g + 1) * D_STATE]
            c_g = xc[rs, D_SSD + SSD_GROUPS * D_STATE + g * D_STATE:
                     D_SSD + SSD_GROUPS * D_STATE + (g + 1) * D_STATE].astype(BF16)
            b_gt = b_g.T.astype(BF16)
            cb = _dot(c_g, b_gt)
            gsl = slice(g * 2 * LANES, (g + 1) * 2 * LANES)
            y_off = _dot(c_g, state[:, gsl].astype(BF16))
            xdt_e = []
            for q in range(2):
                p = 2 * g + q
                psl = slice(p * LANES, (p + 1) * LANES)
                xs_p = xc[rs, psl]
                xdt_p = xs_p * expand_pair(dt_c, p)
                m = []
                for hh in range(2):
                    hd = 2 * p + hh
                    seg = a_cs[:, hd:hd + 1] - a_cs_t[hd:hd + 1, :]
                    dec = jnp.where(causal, jnp.exp(jnp.where(causal, seg, 0.0)), 0.0)
                    m.append((cb * dec).astype(BF16))
                lhs = jnp.concatenate(m, axis=1)
                rhs = jnp.concatenate([jnp.where(left, xdt_p, 0.0), jnp.where(left, 0.0, xdt_p)],
                                      axis=0).astype(BF16)
                y_p = _dot(lhs, rhs) + y_off[:, q * LANES:(q + 1) * LANES] * expand_pair(e_cs, p)
                y_p = y_p + dvec_ref[:, psl] * xs_p
                xdt_e.append((xdt_p * expand_pair(e_end, p)).astype(BF16))
                z_p = proj[SUBLANES + r0:SUBLANES + r0 + CHUNK, OFF_Z + p * LANES:OFF_Z + (p + 1) * LANES]
                xc[rs, psl] = y_p * _silu(z_p)
            dec_row = jnp.concatenate([expand_pair(e_last, 2 * g), expand_pair(e_last, 2 * g + 1)], axis=1)
            upd = _dot(b_gt, jnp.concatenate(xdt_e, axis=1))
            state[:, gsl] = dec_row * state[:, gsl] + upd
            ymix[rs, D_MODEL + g * 2 * LANES:D_MODEL + (g + 1) * 2 * LANES] = _rms(
                xc[rs, gsl], snw_ref[:, gsl]).astype(BF16)
            emit(1)

    emit(len(side))
    h_ref[...] += _dot(ymix[:, D_MODEL:2 * D_MODEL], wout_ref[D_MODEL:2 * D_MODEL, :])
    st_ref[...] = state[...]
    tu_ref[...] = tailu[...]
    tx_ref[...] = tailx[...]


def _mixer_kernel(xa_ref, xb_ref, st0_ref, tu0_ref, tx0_ref, nw_ref, win_ref, cw_ref, cnw_ref, xw_ref, xbias_ref,
                  dtb_ref, alog_ref, dvec_ref, snw_ref, wout_ref,
                  h_ref, st_ref, tu_ref, tx_ref,
                  proj_even, proj_odd, state, tailu, tailx, ubuf, xc, ymix, *, rows, n_inert, tiles_per_seq):
    s = pl.program_id(0)

    @pl.when(s == 0)
    def _():
        proj_odd[...] = jnp.zeros_like(proj_odd)

    @pl.when(jnp.maximum(s - 1, 0) % tiles_per_seq == 0)
    def _():
        state[...] = st0_ref[...]
        tailu[...] = tu0_ref[...]
        tailx[...] = tx0_ref[...]

    mix = functools.partial(
        _mixer_mix, x_ref=xb_ref, cw_ref=cw_ref, cnw_ref=cnw_ref, xw_ref=xw_ref, xb_ref=xbias_ref, dtb_ref=dtb_ref,
        alog_ref=alog_ref, dvec_ref=dvec_ref, snw_ref=snw_ref, wout_ref=wout_ref, h_ref=h_ref, st_ref=st_ref,
        tu_ref=tu_ref, tx_ref=tx_ref, state=state, tailu=tailu, tailx=tailx, ubuf=ubuf, xc=xc, ymix=ymix,
        rows=rows, n_inert=n_inert)

    @pl.when(s % 2 == 0)
    def _():
        mix(proj_odd, _mixer_project(xa_ref, nw_ref, win_ref, proj_even, rows))

    @pl.when(s % 2 == 1)
    def _():
        mix(proj_even, _mixer_project(xa_ref, nw_ref, win_ref, proj_odd, rows))


def _mixer(x, st0, tu0, tx0, wts, *, rows, n_inert):
    bsz, length, _ = x.shape
    tiles_per_seq = length // rows
    n_tiles = bsz * tiles_per_seq
    const = lambda shp: pl.BlockSpec(shp, lambda s: (0,) * len(shp))
    resident = lambda shp: pl.BlockSpec(shp, lambda s: (0,) * len(shp), pipeline_mode=pl.Buffered(1))
    projected = lambda s: jnp.minimum(s, n_tiles - 1)
    mixed = lambda s: jnp.maximum(s - 1, 0)
    (nw, win, cw, cnw, xw, xb, dtb, alog, dvec, snw, wout) = wts
    kern = functools.partial(_mixer_kernel, rows=rows, n_inert=n_inert, tiles_per_seq=tiles_per_seq)
    return pl.pallas_call(
        kern,
        grid=(n_tiles + 1,),
        in_specs=[
            pl.BlockSpec((None, rows, D_MODEL),
                         lambda s: (projected(s) // tiles_per_seq, projected(s) % tiles_per_seq, 0)),
            pl.BlockSpec((None, rows, D_MODEL), lambda s: (mixed(s) // tiles_per_seq, mixed(s) % tiles_per_seq, 0)),
            const((D_STATE, D_SSD)), const((SUBLANES, D_MODEL)), const((SUBLANES, D_XBC)),
            const((1, D_MODEL)), resident((D_MODEL, D_IN_PAD)), const((SUBLANES, D_MODEL)), const((1, D_MODEL)),
            const((SUBLANES, D_XBC)), const((1, D_XBC)), const((1, LANES)), const((1, LANES)),
            const((1, D_SSD)), const((1, D_SSD)), resident((2 * D_MODEL, D_MODEL)),
        ],
        out_specs=[
            pl.BlockSpec((None, rows, D_MODEL), lambda s: (mixed(s) // tiles_per_seq, mixed(s) % tiles_per_seq, 0)),
            pl.BlockSpec((None, D_STATE, D_SSD), lambda s: (mixed(s) // tiles_per_seq, 0, 0)),
            pl.BlockSpec((None, SUBLANES, D_MODEL), lambda s: (mixed(s) // tiles_per_seq, 0, 0)),
            pl.BlockSpec((None, SUBLANES, D_XBC), lambda s: (mixed(s) // tiles_per_seq, 0, 0)),
        ],
        out_shape=[
            jax.ShapeDtypeStruct((bsz, length, D_MODEL), F32),
            jax.ShapeDtypeStruct((bsz, D_STATE, D_SSD), F32),
            jax.ShapeDtypeStruct((bsz, SUBLANES, D_MODEL), F32),
            jax.ShapeDtypeStruct((bsz, SUBLANES, D_XBC), F32),
        ],
        scratch_shapes=[
            pltpu.VMEM((rows + SUBLANES, D_IN_PAD), F32),
            pltpu.VMEM((rows + SUBLANES, D_IN_PAD), F32),
            pltpu.VMEM((D_STATE, D_SSD), F32),
            pltpu.VMEM((SUBLANES, D_MODEL), F32),
            pltpu.VMEM((SUBLANES, D_XBC), F32),
            pltpu.VMEM((rows + SUBLANES, D_MODEL), F32),
            pltpu.VMEM((rows, D_XBC), F32),
            pltpu.VMEM((rows, 2 * D_MODEL), BF16),
        ],
        compiler_params=pltpu.CompilerParams(dimension_semantics=("arbitrary",), vmem_limit_bytes=VMEM_LIMIT),
        name="mixer",
    )(x, x, st0, tu0, tx0, nw, win, cw, cnw, xw, xb, dtb, alog, dvec, snw, wout)


def _mixer_weights(norm_mix_w, w_in, conv_short_w, conv_norm_w, ssd_conv_w, ssd_conv_b, ssd_dt_bias,
                   ssd_a_log, ssd_d, ssd_norm_w, w_out):
    pad_rows = lambda w: jnp.pad(w, ((0, SUBLANES - w.shape[0]), (0, 0)))
    pad_lanes = lambda v: jnp.pad(v, (0, LANES - v.shape[0]))[None, :]
    win = jnp.pad(w_in, ((0, 0), (0, D_IN_PAD - w_in.shape[1]))).astype(BF16)
    return (norm_mix_w[None, :], win, pad_rows(conv_short_w), conv_norm_w[None, :], pad_rows(ssd_conv_w),
            ssd_conv_b[None, :], pad_lanes(ssd_dt_bias), pad_lanes(ssd_a_log),
            jnp.repeat(ssd_d, SSD_HEAD_DIM)[None, :], ssd_norm_w[None, :], w_out.astype(BF16))


def _run_mixer(x, meta_tokens, wts):
    zeros = lambda *s: jnp.zeros(s, F32)
    meta = jnp.concatenate([zeros(CHUNK - N_META, D_MODEL), meta_tokens.astype(F32)], axis=0)[None]
    _, st0, tu0, tx0 = _mixer(meta, zeros(D_STATE, D_SSD), zeros(SUBLANES, D_MODEL), zeros(SUBLANES, D_XBC),
                              wts, rows=CHUNK, n_inert=CHUNK - N_META)
    h, _, _, _ = _mixer(x, st0[0], tu0[0], tx0[0], wts, rows=MIX_ROWS, n_inert=0)
    return h


def _row_tiles_store(ref, val, rows):
    for j in range(LANES_PER_ROW):
        ref[pl.ds(j, rows, stride=SUBLANES), :] = val[:, j * LANES:(j + 1) * LANES]


def _row_tiles_load(ref, first_row, rows):
    return jnp.concatenate(
        [ref[pl.ds(first_row * SUBLANES + j, rows, stride=SUBLANES), :] for j in range(LANES_PER_ROW)], axis=1)


def _router_kernel(h_ref, nw_ref, rw_ref, rb_ref, xn_ref, idx_ref, rank_ref, gate_ref, cnt_ref, carry, *, rows):
    @pl.when(pl.program_id(0) == 0)
    def _():
        carry[...] = jnp.zeros_like(carry)

    xn = _rms(h_ref[...], nw_ref[...])
    _row_tiles_store(xn_ref, xn, rows)

    lane = lax.broadcasted_iota(I32, (rows, LANES), 1).astype(F32)
    x_hi = xn.astype(BF16)
    x_lo = (xn - x_hi.astype(F32)).astype(BF16)
    w = rw_ref[...]
    w_hi = w.astype(BF16)
    w_lo = (w - w_hi.astype(F32)).astype(BF16)
    logits = _dot(x_hi, w_hi) + _dot(x_lo, w_hi) + _dot(x_hi, w_lo) + rb_ref[...]
    cur = jnp.where(lane < N_EXPERTS, logits, -jnp.inf)
    idxs, vals = [], []
    for _ in range(TOP_K):
        m = jnp.max(cur, axis=1, keepdims=True)
        ik = jnp.min(jnp.where(cur == m, lane, float(LANES)), axis=1, keepdims=True)
        cur = jnp.where(lane == ik, -jnp.inf, cur)
        idxs.append(ik)
        vals.append(m)
    es = [jnp.exp(v - vals[0]) for v in vals]
    den = es[0] + es[1] + es[2] + es[3]
    gates = [e / den for e in es]

    onehot = jnp.zeros((rows, LANES), F32)
    for ik in idxs:
        onehot = onehot + (lane == ik).astype(F32)
    r_i = lax.broadcasted_iota(I32, (rows, rows), 0)
    c_i = lax.broadcasted_iota(I32, (rows, rows), 1)
    before = (r_i > c_i).astype(BF16)
    rankmat = _dot(before, onehot.astype(BF16)) + carry[...]
    carry[...] = carry[...] + jnp.sum(onehot, axis=0, keepdims=True)
    ranks = [jnp.sum(jnp.where(lane == ik, rankmat, 0.0), axis=1, keepdims=True) for ik in idxs]

    def pack(cols):
        m = jnp.zeros((rows, LANES), F32)
        for k, col in enumerate(cols):
            m = jnp.where(lane == k, col, m)
        return m

    idx_ref[...] = pack(idxs).T[0:SUBLANES, :].astype(I32)
    rank_ref[...] = pack(ranks).T[0:SUBLANES, :].astype(I32)
    gate_ref[...] = pack(gates)
    cnt_ref[...] = jnp.broadcast_to(carry[...], (SUBLANES, LANES))


def _router(h2, nw, rw, rb):
    n_tok = h2.shape[0]
    rows = ROUTE_ROWS
    const = lambda shp: pl.BlockSpec(shp, lambda i: (0,) * len(shp))
    return pl.pallas_call(
        functools.partial(_router_kernel, rows=rows),
        grid=(n_tok // rows,),
        in_specs=[pl.BlockSpec((rows, D_MODEL), lambda i: (i, 0)), const((1, D_MODEL)),
                  const((D_MODEL, LANES)), const((1, LANES))],
        out_specs=[
            pl.BlockSpec((rows * SUBLANES, LANES), lambda i: (i, 0)),
            pl.BlockSpec((SUBLANES, rows), lambda i: (0, i)),
            pl.BlockSpec((SUBLANES, rows), lambda i: (0, i)),
            pl.BlockSpec((rows, LANES), lambda i: (i, 0)),
            const((SUBLANES, LANES)),
        ],
        out_shape=[
            jax.ShapeDtypeStruct((n_tok * SUBLANES, LANES), F32),
            jax.ShapeDtypeStruct((SUBLANES, n_tok), I32),
            jax.ShapeDtypeStruct((SUBLANES, n_tok), I32),
            jax.ShapeDtypeStruct((n_tok, LANES), F32),
            jax.ShapeDtypeStruct((SUBLANES, LANES), F32),
        ],
        scratch_shapes=[pltpu.VMEM((1, LANES), F32)],
        compiler_params=pltpu.CompilerParams(dimension_semantics=("arbitrary",), vmem_limit_bytes=VMEM_LIMIT),
        name="router",
    )(h2, nw, rw, rb)


def _dest_kernel(start_ref, idx_ref, rank_ref, dest_ref):
    idx = idx_ref[...]
    dest = rank_ref[...]
    for e in range(N_EXPERTS):
        dest = dest + jnp.where(idx == e, start_ref[e], 0)
    dest_ref[...] = dest


def _dest(pad_start, idx_t, rank_t):
    n_tok = idx_t.shape[1]
    cols = min(n_tok, 4096)
    spec = pl.BlockSpec((SUBLANES, cols), lambda i, s: (0, i))
    return pl.pallas_call(
        _dest_kernel,
        grid_spec=pltpu.PrefetchScalarGridSpec(
            num_scalar_prefetch=1, grid=(n_tok // cols,), in_specs=[spec, spec], out_specs=spec),
        out_shape=jax.ShapeDtypeStruct((SUBLANES, n_tok), I32),
        name="dest",
    )(pad_start, idx_t, rank_t)


def _slot_rows_kernel(zstart_ref, zcount_ref, zoff_ref, dest_ref, rows_ref, *, n_tok):
    i = pl.program_id(0)
    base = i * SLOT_CHUNK

    def body(g, c):
        for u in range(ISSUE_UNROLL):
            a = g * ISSUE_UNROLL + u
            flat = base + a
            rows_ref[dest_ref[a]] = (flat & (TOP_K - 1)) * n_tok + (flat >> 2)
        return c

    lax.fori_loop(0, SLOT_CHUNK // ISSUE_UNROLL, body, 0)

    @pl.when(i == pl.num_programs(0) - 1)
    def _():
        def fill_segment(s, c):
            first = zstart_ref[s]
            spare = n_tok * TOP_K + zoff_ref[s]

            def fill(j, c2):
                rows_ref[first + j] = spare + j
                return c2

            lax.fori_loop(0, zcount_ref[s], fill, 0)
            return c

        lax.fori_loop(0, N_EXPERTS + 1, fill_segment, 0)


def _slot_rows(zstart, zcount, zoff, dest_flat, n_rows, n_tok):
    return pl.pallas_call(
        functools.partial(_slot_rows_kernel, n_tok=n_tok),
        grid_spec=pltpu.PrefetchScalarGridSpec(
            num_scalar_prefetch=3, grid=(dest_flat.shape[0] // SLOT_CHUNK,),
            in_specs=[pl.BlockSpec((SLOT_CHUNK,), lambda i, a, b, c: (i,), memory_space=pltpu.SMEM)],
            out_specs=pl.BlockSpec(memory_space=pltpu.SMEM)),
        out_shape=jax.ShapeDtypeStruct((n_rows,), I32),
        compiler_params=pltpu.CompilerParams(dimension_semantics=("arbitrary",)),
        name="slot_rows",
    )(zstart, zcount, zoff, dest_flat)


def _tile_rows(row):
    return pl.ds(pl.multiple_of(row * SUBLANES, SUBLANES), SUBLANES)


def _expert_mlp(x_bf, wgu_bf, bgu_ref, wd_bf, bd_ref, side):
    side = list(side)
    n_pieces = 3 * (D_FF // FF_COLS)
    per_piece = -(-len(side) // n_pieces)

    def emit():
        for _ in range(per_piece):
            if side:
                side.pop(0)()

    y = bd_ref[...]
    for j in range(D_FF // FF_COLS):
        c_glu = slice(j * FF_COLS, (j + 1) * FF_COLS)
        c_lin = slice(D_FF + j * FF_COLS, D_FF + (j + 1) * FF_COLS)
        glu = jnp.minimum(_dot(x_bf, wgu_bf[:, c_glu]) + bgu_ref[:, c_glu], SWIGLU_LIMIT)
        emit()
        lin = jnp.clip(_dot(x_bf, wgu_bf[:, c_lin]) + bgu_ref[:, c_lin], -SWIGLU_LIMIT, SWIGLU_LIMIT)
        emit()
        act = glu * (1.0 / (1.0 + jnp.exp(-SWIGLU_ALPHA * glu))) * (lin + 1.0)
        y = y + _dot(act.astype(BF16), wd_bf[c_glu, :])
        emit()
    return y


def _expert_kernel(be_ref, rows_cur, rows_next, rows_prev, xn_hbm, wgu_ref, bgu_ref, wd_ref, bd_ref, ytok_hbm,
                   xbuf, ybuf, wgu_bf, wd_bf, gsem, ssem, *, rows, n_blocks, n_tok):
    b = pl.program_id(0)
    cur = lax.rem(b, 2)
    oth = 1 - cur
    blk = rows * SUBLANES

    def gather_starts(slot_rows, buf):
        def thunk(r):
            def run():
                tok = slot_rows[r] & (n_tok - 1)
                pltpu.make_async_copy(xn_hbm.at[_tile_rows(tok), :], xbuf.at[buf, pl.ds(r * SUBLANES, SUBLANES), :],
                                      gsem.at[buf]).start(priority=r % 2)
            return run
        return [thunk(r) for r in range(rows)]

    def scatter_starts(buf):
        def thunk(r):
            def run():
                pltpu.make_async_copy(ybuf.at[buf, pl.ds(r * SUBLANES, SUBLANES), :],
                                      ytok_hbm.at[_tile_rows(rows_prev[r]), :], ssem.at[buf]).start(priority=r % 2)
            return run
        return [thunk(r) for r in range(rows)]

    def gather_wait(buf):
        pltpu.make_async_copy(xn_hbm.at[pl.ds(0, blk), :], xbuf.at[buf], gsem.at[buf]).wait()

    def scatter_wait(buf):
        pltpu.make_async_copy(ybuf.at[buf], ytok_hbm.at[pl.ds(0, blk), :], ssem.at[buf]).wait()

    def cast_weights_if_new_expert():
        @pl.when((b == 0) | (be_ref[b] != be_ref[jnp.maximum(b - 1, 0)]))
        def _():
            for r in range(0, D_MODEL, CAST_ROWS):
                wgu_bf[r:r + CAST_ROWS, :] = wgu_ref[r:r + CAST_ROWS, :].astype(BF16)
                wd_bf[r:r + CAST_ROWS, :] = wd_ref[r:r + CAST_ROWS, :].astype(BF16)

    def run_block(side):
        x_bf = _row_tiles_load(xbuf.at[cur], 0, rows).astype(BF16)
        return _expert_mlp(x_bf, wgu_bf, bgu_ref, wd_bf, bd_ref, side)

    @pl.when(b == 0)
    def _():
        for start in gather_starts(rows_cur, 0):
            start()
        cast_weights_if_new_expert()
        gather_wait(0)
        y = run_block(gather_starts(rows_next, 1))
        _row_tiles_store(ybuf.at[0], y, rows)

    @pl.when((b > 0) & (b < n_blocks))
    def _():
        cast_weights_if_new_expert()
        gather_wait(cur)
        y = run_block(gather_starts(rows_next, oth) + scatter_starts(oth))

        @pl.when(b >= 2)
        def _():
            scatter_wait(cur)

        _row_tiles_store(ybuf.at[cur], y, rows)

    @pl.when(b == n_blocks)
    def _():
        for start in scatter_starts(oth):
            start()
        scatter_wait(oth)
        scatter_wait(cur)
        gather_wait(cur)


def _experts(blk_expert, slot_rows, xn2, wgu, bgu, wd, bd):
    rows = MOE_BLOCK
    n_rows = slot_rows.shape[0]
    n_blocks = n_rows // rows
    n_tok = xn2.shape[0] // SUBLANES
    last = n_blocks - 1
    by_expert = lambda shp: pl.BlockSpec((None,) + shp, lambda b, be: (be[jnp.minimum(b, last)], 0, 0))
    slot_block = lambda shift: pl.BlockSpec((rows,), lambda b, be: (jnp.clip(b + shift, 0, last),),
                                            memory_space=pltpu.SMEM)
    return pl.pallas_call(
        functools.partial(_expert_kernel, rows=rows, n_blocks=n_blocks, n_tok=n_tok),
        grid_spec=pltpu.PrefetchScalarGridSpec(
            num_scalar_prefetch=1, grid=(n_blocks + 1,),
            in_specs=[slot_block(0), slot_block(1), slot_block(-1), pl.BlockSpec(memory_space=pl.ANY),
                      by_expert((D_MODEL, 2 * D_FF)), by_expert((1, 2 * D_FF)),
                      by_expert((D_FF, D_MODEL)), by_expert((1, D_MODEL))],
            out_specs=pl.BlockSpec(memory_space=pl.ANY),
            scratch_shapes=[pltpu.VMEM((2, rows * SUBLANES, LANES), F32), pltpu.VMEM((2, rows * SUBLANES, LANES), F32),
                            pltpu.VMEM((D_MODEL, 2 * D_FF), BF16), pltpu.VMEM((D_FF, D_MODEL), BF16),
                            pltpu.SemaphoreType.DMA((2,)), pltpu.SemaphoreType.DMA((2,))]),
        out_shape=jax.ShapeDtypeStruct((n_rows * SUBLANES, LANES), F32),
        compiler_params=pltpu.CompilerParams(dimension_semantics=("arbitrary",), vmem_limit_bytes=VMEM_LIMIT,
                                             disable_bounds_checks=True),
        name="experts",
    )(blk_expert, slot_rows, slot_rows, slot_rows, xn2, wgu, bgu, wd, bd)


def _combine_kernel(h_ref, gate_ref, fw_ref, y0_ref, y1_ref, y2_ref, y3_ref, out_ref, *, tokens):
    acc = h_ref[...]
    gate = gate_ref[...]
    for k, y_ref in enumerate((y0_ref, y1_ref, y2_ref, y3_ref)):
        acc = acc + gate[:, k:k + 1] * _row_tiles_load(y_ref, 0, tokens)
    out_ref[...] = _rms(acc, fw_ref[...])


def _combine(h2, gates, fw, ytok):
    n_tok = h2.shape[0]
    tokens = COMBINE_ROWS
    n_tiles = n_tok // tokens
    y_spec = lambda k: pl.BlockSpec((tokens * SUBLANES, LANES), lambda i: (k * n_tiles + i, 0))
    return pl.pallas_call(
        functools.partial(_combine_kernel, tokens=tokens),
        grid=(n_tiles,),
        in_specs=[pl.BlockSpec((tokens, D_MODEL), lambda i: (i, 0)),
                  pl.BlockSpec((tokens, LANES), lambda i: (i, 0)),
                  pl.BlockSpec((1, D_MODEL), lambda i: (0, 0))] + [y_spec(k) for k in range(TOP_K)],
        out_specs=pl.BlockSpec((tokens, D_MODEL), lambda i: (i, 0)),
        out_shape=jax.ShapeDtypeStruct((n_tok, D_MODEL), F32),
        compiler_params=pltpu.CompilerParams(dimension_semantics=("arbitrary",), vmem_limit_bytes=VMEM_LIMIT),
        name="combine",
    )(h2, gates, fw, ytok, ytok, ytok, ytok)


def _moe(h2, norm_ffn_w, router_w, router_b, w_gate_up, b_gate_up, w_down, b_down, norm_final_w):
    n_tok = h2.shape[0]
    n_assign = n_tok * TOP_K
    n_blocks = -(-(n_assign + N_EXPERTS * (MOE_BLOCK - 1)) // MOE_BLOCK)
    n_rows = n_blocks * MOE_BLOCK

    rw = jnp.pad(router_w, ((0, 0), (0, LANES - N_EXPERTS)))
    rb = jnp.pad(router_b, (0, LANES - N_EXPERTS))[None, :]
    xn2, idx_t, rank_t, gates, cnt = _router(h2, norm_ffn_w[None, :], rw, rb)

    counts = cnt[0, :N_EXPERTS].astype(I32)
    padded = ((counts + MOE_BLOCK - 1) // MOE_BLOCK) * MOE_BLOCK
    pad_end = jnp.cumsum(padded)
    pad_start = pad_end - padded
    blk_first_row = jnp.arange(n_blocks, dtype=I32) * MOE_BLOCK
    blk_expert = jnp.minimum(jnp.sum((pad_end[None, :] <= blk_first_row[:, None]).astype(I32), axis=1),
                             N_EXPERTS - 1)
    zstart = jnp.concatenate([pad_start + counts, pad_end[-1:]]).astype(I32)
    zcount = jnp.concatenate([padded - counts, n_rows - pad_end[-1:]]).astype(I32)
    zoff = (jnp.cumsum(zcount) - zcount).astype(I32)

    dest_t = _dest(pad_start.astype(I32), idx_t, rank_t)
    dest_flat = dest_t[:TOP_K].T.reshape(-1)
    slot_rows = _slot_rows(zstart, zcount, zoff, dest_flat, n_rows, n_tok)

    ytok = _experts(blk_expert, slot_rows, xn2, w_gate_up, b_gate_up[:, None, :], w_down, b_down[:, None, :])
    return _combine(h2, gates, norm_final_w[None, :], ytok)


def kernel(x, meta_tokens, norm_mix_w, w_in, conv_short_w, conv_norm_w, ssd_conv_w, ssd_conv_b, ssd_dt_bias,
           ssd_a_log, ssd_d, ssd_norm_w, w_out, norm_ffn_w, router_w, router_b, w_gate_up, b_gate_up, w_down,
           b_down, norm_final_w):
    bsz, seq, _ = x.shape
    wts = _mixer_weights(norm_mix_w[0], w_in[0], conv_short_w[0], conv_norm_w[0], ssd_conv_w[0], ssd_conv_b[0],
                         ssd_dt_bias[0], ssd_a_log[0], ssd_d[0], ssd_norm_w[0], w_out[0])
    h = _run_mixer(x, meta_tokens, wts)
    out = _moe(h.reshape(bsz * seq, D_MODEL), norm_ffn_w[0], router_w[0], router_b[0], w_gate_up[0],
               b_gate_up[0], w_down[0], b_down[0], norm_final_w)
    return out.reshape(bsz, seq, D_MODEL)
```

```python
import functools

import jax
import jax.numpy as jnp
from jax import lax
from jax.experimental import pallas as pl
from jax.experimental.pallas import tpu as pltpu

F32 = jnp.float32
BF16 = jnp.bfloat16
I32 = jnp.int32

D_MODEL = 1024
N_META = 16
CONV_WIDTH = 3
CONV_HEADS = 8
SSD_HEADS = 16
SSD_HEAD_DIM = 64
D_SSD = SSD_HEADS * SSD_HEAD_DIM
SSD_GROUPS = 4
D_STATE = 128
SSD_CONV = 4
CHUNK = 128
N_EXPERTS = 32
TOP_K = 4
D_FF = 1024
SWIGLU_ALPHA = 1.702
SWIGLU_LIMIT = 7.0
EPS = 1e-5

LANES = 128
SUBLANES = 8
VMEM_LIMIT = 56 * 1024 * 1024

OFF_CB, OFF_CC, OFF_CX, OFF_Z, OFF_XBC, OFF_DT = 0, 1024, 2048, 3072, 4096, 6144
D_XBC = D_SSD + 2 * SSD_GROUPS * D_STATE
D_IN_PAD = OFF_DT + LANES
LANES_PER_ROW = D_MODEL // LANES

MIX_ROWS = 256
PROJ_COLS = 512
MIX_COLS = 512
ROUTE_ROWS = 256
MOE_BLOCK = 256
FF_COLS = 512
COMBINE_ROWS = 512
SLOT_CHUNK = 4096
ISSUE_UNROLL = 16
CAST_ROWS = 128


def _rms(x, w):
    return x * lax.rsqrt(jnp.mean(x * x, axis=-1, keepdims=True) + EPS) * w


def _silu(x):
    return x * (1.0 / (1.0 + jnp.exp(-x)))


def _dot(a, b):
    return jnp.dot(a, b, preferred_element_type=F32)


def _split3(x):
    hi = x.astype(BF16)
    rest = x - hi.astype(F32)
    mid = rest.astype(BF16)
    lo = (rest - mid.astype(F32)).astype(BF16)
    return hi, mid, lo


def _mixer_project(x_ref, nw_ref, win_ref, proj, rows):
    n_bf = _rms(x_ref[...], nw_ref[...]).astype(BF16)

    def chunk(c0):
        c1 = min(c0 + PROJ_COLS, D_IN_PAD)

        def run():
            proj[SUBLANES:SUBLANES + rows, c0:c1] = _dot(n_bf, win_ref[:, c0:c1])

        return run

    return [chunk(c0) for c0 in range(0, D_IN_PAD, PROJ_COLS)]


def _mixer_mix(proj, side, x_ref, cw_ref, cnw_ref, xw_ref, xb_ref, dtb_ref, alog_ref, dvec_ref, snw_ref, wout_ref,
               h_ref, st_ref, tu_ref, tx_ref, state, tailu, tailx, ubuf, xc, ymix, *, rows, n_inert):
    side = list(side)

    def emit(n):
        for _ in range(n):
            if side:
                side.pop(0)()

    n_chunks = rows // CHUNK
    body = slice(SUBLANES, SUBLANES + rows)
    proj[0:SUBLANES, OFF_XBC:OFF_XBC + D_XBC] = tailx[...]
    ubuf[0:SUBLANES, :] = tailu[...]

    for c0 in range(0, D_MODEL, MIX_COLS):
        cs = slice(c0, c0 + MIX_COLS)
        ubuf[body, cs] = proj[body, OFF_CC + c0:OFF_CC + c0 + MIX_COLS] * proj[body, OFF_CX + c0:OFF_CX + c0 + MIX_COLS]
        conv = cw_ref[0:1, cs] * ubuf[SUBLANES - 2:SUBLANES - 2 + rows, cs]
        conv = conv + cw_ref[1:2, cs] * ubuf[SUBLANES - 1:SUBLANES - 1 + rows, cs]
        conv = conv + cw_ref[2:3, cs] * ubuf[SUBLANES:SUBLANES + rows, cs]
        yc = proj[body, OFF_CB + c0:OFF_CB + c0 + MIX_COLS] * conv
        for g in range(MIX_COLS // LANES):
            sl = slice(c0 + g * LANES, c0 + (g + 1) * LANES)
            ymix[:, sl] = _rms(yc[:, g * LANES:(g + 1) * LANES], cnw_ref[:, sl]).astype(BF16)
        tailu[:, cs] = ubuf[rows:rows + SUBLANES, cs]
        emit(1)
    h_ref[...] = x_ref[...] + _dot(ymix[:, 0:D_MODEL], wout_ref[0:D_MODEL, :])

    for c0 in range(0, D_XBC, MIX_COLS):
        cs = slice(c0, c0 + MIX_COLS)
        ps = slice(OFF_XBC + c0, OFF_XBC + c0 + MIX_COLS)
        acc = xb_ref[:, cs] + xw_ref[0:1, cs] * proj[SUBLANES - 3:SUBLANES - 3 + rows, ps]
        for k in range(1, SSD_CONV):
            acc = acc + xw_ref[k:k + 1, cs] * proj[SUBLANES - 3 + k:SUBLANES - 3 + k + rows, ps]
        xc[:, cs] = _silu(acc)
        tailx[:, cs] = proj[rows:rows + SUBLANES, ps]
        emit(1)

    dt_raw = proj[body, OFF_DT:OFF_DT + LANES] + dtb_ref[...]
    dt = jnp.maximum(dt_raw, 0.0) + jnp.log(1.0 + jnp.exp(-jnp.abs(dt_raw)))
    if n_inert:
        r_all = lax.broadcasted_iota(I32, (rows, LANES), 0)
        dt = jnp.where(r_all >= n_inert, dt, 0.0)
    a_neg = -jnp.exp(alog_ref[...])
    da = dt * a_neg

    row = lax.broadcasted_iota(I32, (CHUNK, CHUNK), 0)
    lane = lax.broadcasted_iota(I32, (CHUNK, CHUNK), 1)
    causal = row >= lane
    tri = causal.astype(BF16)
    left = lane < SSD_HEAD_DIM

    def expand_pair(v, p):
        return jnp.where(left, v[:, 2 * p:2 * p + 1], v[:, 2 * p + 1:2 * p + 2])

    for c in range(n_chunks):
        r0 = c * CHUNK
        rs = slice(r0, r0 + CHUNK)
        dt_c = dt[rs, :]
        da_hi, da_mid, da_lo = _split3(da[rs, :])
        a_cs = _dot(tri, da_hi) + _dot(tri, da_mid) + _dot(tri, da_lo)
        a_cs_t = a_cs.T
        a_last = a_cs[CHUNK - 1:CHUNK, :]
        e_cs = jnp.exp(a_cs)
        e_end = jnp.exp(a_last - a_cs)
        e_last = jnp.exp(a_last)
        for g in range(SSD_GROUPS):
            b_g = xc[rs, D_SSD + g * D_STATE:D_SSD + (g + 1) * D_STATE]
            c_g = xc[rs, D_SSD + SSD_GROUPS * D_STATE + g * D_STATE:
                     D_SSD + SSD_GROUPS * D_STATE + (g + 1) * D_STATE].astype(BF16)
            b_gt = b_g.T.astype(BF16)
            cb = _dot(c_g, b_gt)
            gsl = slice(g * 2 * LANES, (g + 1) * 2 * LANES)
            y_off = _dot(c_g, state[:, gsl].astype(BF16))
            xdt_e = []
            for q in range(2):
                p = 2 * g + q
                psl = slice(p * LANES, (p + 1) * LANES)
                xs_p = xc[rs, psl]
                xdt_p = xs_p * expand_pair(dt_c, p)
                m = []
                for hh in range(2):
                    hd = 2 * p + hh
                    seg = a_cs[:, hd:hd + 1] - a_cs_t[hd:hd + 1, :]
                    dec = jnp.where(causal, jnp.exp(jnp.where(causal, seg, 0.0)), 0.0)
                    m.append((cb * dec).astype(BF16))
                lhs = jnp.concatenate(m, axis=1)
                rhs = jnp.concatenate([jnp.where(left, xdt_p, 0.0), jnp.where(left, 0.0, xdt_p)],
                                      axis=0).astype(BF16)
                y_p = _dot(lhs, rhs) + y_off[:, q * LANES:(q + 1) * LANES] * expand_pair(e_cs, p)
                y_p = y_p + dvec_ref[:, psl] * xs_p
                xdt_e.append((xdt_p * expand_pair(e_end, p)).astype(BF16))
                z_p = proj[SUBLANES + r0:SUBLANES + r0 + CHUNK, OFF_Z + p * LANES:OFF_Z + (p + 1) * LANES]
                xc[rs, psl] = y_p * _silu(z_p)
            dec_row = jnp.concatenate([expand_pair(e_last, 2 * g), expand_pair(e_last, 2 * g + 1)], axis=1)
            upd = _dot(b_gt, jnp.concatenate(xdt_e, axis=1))
            state[:, gsl] = dec_row * state[:, gsl] + upd
            ymix[rs, D_MODEL + g * 2 * LANES:D_MODEL + (g + 1) * 2 * LANES] = _rms(
                xc[rs, gsl], snw_ref[:, gsl]).astype(BF16)
            emit(1)

    emit(len(side))
    h_ref[...] += _dot(ymix[:, D_MODEL:2 * D_MODEL], wout_ref[D_MODEL:2 * D_MODEL, :])
    st_ref[...] = state[...]
    tu_ref[...] = tailu[...]
    tx_ref[...] = tailx[...]


def _mixer_kernel(xa_ref, xb_ref, st0_ref, tu0_ref, tx0_ref, nw_ref, win_ref, cw_ref, cnw_ref, xw_ref, xbias_ref,
                  dtb_ref, alog_ref, dvec_ref, snw_ref, wout_ref,
                  h_ref, st_ref, tu_ref, tx_ref,
                  proj_even, proj_odd, state, tailu, tailx, ubuf, xc, ymix, *, rows, n_inert, tiles_per_seq):
    s = pl.program_id(0)

    @pl.when(s == 0)
    def _():
        proj_odd[...] = jnp.zeros_like(proj_odd)

    @pl.when(jnp.maximum(s - 1, 0) % tiles_per_seq == 0)
    def _():
        state[...] = st0_ref[...]
        tailu[...] = tu0_ref[...]
        tailx[...] = tx0_ref[...]

    mix = functools.partial(
        _mixer_mix, x_ref=xb_ref, cw_ref=cw_ref, cnw_ref=cnw_ref, xw_ref=xw_ref, xb_ref=xbias_ref, dtb_ref=dtb_ref,
        alog_ref=alog_ref, dvec_ref=dvec_ref, snw_ref=snw_ref, wout_ref=wout_ref, h_ref=h_ref, st_ref=st_ref,
        tu_ref=tu_ref, tx_ref=tx_ref, state=state, tailu=tailu, tailx=tailx, ubuf=ubuf, xc=xc, ymix=ymix,
        rows=rows, n_inert=n_inert)

    @pl.when(s % 2 == 0)
    def _():
        mix(proj_odd, _mixer_project(xa_ref, nw_ref, win_ref, proj_even, rows))

    @pl.when(s % 2 == 1)
    def _():
        mix(proj_even, _mixer_project(xa_ref, nw_ref, win_ref, proj_odd, rows))


def _mixer(x, st0, tu0, tx0, wts, *, rows, n_inert):
    bsz, length, _ = x.shape
    tiles_per_seq = length // rows
    n_tiles = bsz * tiles_per_seq
    const = lambda shp: pl.BlockSpec(shp, lambda s: (0,) * len(shp))
    resident = lambda shp: pl.BlockSpec(shp, lambda s: (0,) * len(shp), pipeline_mode=pl.Buffered(1))
    projected = lambda s: jnp.minimum(s, n_tiles - 1)
    mixed = lambda s: jnp.maximum(s - 1, 0)
    (nw, win, cw, cnw, xw, xb, dtb, alog, dvec, snw, wout) = wts
    kern = functools.partial(_mixer_kernel, rows=rows, n_inert=n_inert, tiles_per_seq=tiles_per_seq)
    return pl.pallas_call(
        kern,
        grid=(n_tiles + 1,),
        in_specs=[
            pl.BlockSpec((None, rows, D_MODEL),
                         lambda s: (projected(s) // tiles_per_seq, projected(s) % tiles_per_seq, 0)),
            pl.BlockSpec((None, rows, D_MODEL), lambda s: (mixed(s) // tiles_per_seq, mixed(s) % tiles_per_seq, 0)),
            const((D_STATE, D_SSD)), const((SUBLANES, D_MODEL)), const((SUBLANES, D_XBC)),
            const((1, D_MODEL)), resident((D_MODEL, D_IN_PAD)), const((SUBLANES, D_MODEL)), const((1, D_MODEL)),
            const((SUBLANES, D_XBC)), const((1, D_XBC)), const((1, LANES)), const((1, LANES)),
            const((1, D_SSD)), const((1, D_SSD)), resident((2 * D_MODEL, D_MODEL)),
        ],
        out_specs=[
            pl.BlockSpec((None, rows, D_MODEL), lambda s: (mixed(s) // tiles_per_seq, mixed(s) % tiles_per_seq, 0)),
            pl.BlockSpec((None, D_STATE, D_SSD), lambda s: (mixed(s) // tiles_per_seq, 0, 0)),
            pl.BlockSpec((None, SUBLANES, D_MODEL), lambda s: (mixed(s) // tiles_per_seq, 0, 0)),
            pl.BlockSpec((None, SUBLANES, D_XBC), lambda s: (mixed(s) // tiles_per_seq, 0, 0)),
        ],
        out_shape=[
            jax.ShapeDtypeStruct((bsz, length, D_MODEL), F32),
            jax.ShapeDtypeStruct((bsz, D_STATE, D_SSD), F32),
            jax.ShapeDtypeStruct((bsz, SUBLANES, D_MODEL), F32),
            jax.ShapeDtypeStruct((bsz, SUBLANES, D_XBC), F32),
        ],
        scratch_shapes=[
            pltpu.VMEM((rows + SUBLANES, D_IN_PAD), F32),
            pltpu.VMEM((rows + SUBLANES, D_IN_PAD), F32),
            pltpu.VMEM((D_STATE, D_SSD), F32),
            pltpu.VMEM((SUBLANES, D_MODEL), F32),
            pltpu.VMEM((SUBLANES, D_XBC), F32),
            pltpu.VMEM((rows + SUBLANES, D_MODEL), F32),
            pltpu.VMEM((rows, D_XBC), F32),
            pltpu.VMEM((rows, 2 * D_MODEL), BF16),
        ],
        compiler_params=pltpu.CompilerParams(dimension_semantics=("arbitrary",), vmem_limit_bytes=VMEM_LIMIT),
        name="mixer",
    )(x, x, st0, tu0, tx0, nw, win, cw, cnw, xw, xb, dtb, alog, dvec, snw, wout)


def _mixer_weights(norm_mix_w, w_in, conv_short_w, conv_norm_w, ssd_conv_w, ssd_conv_b, ssd_dt_bias,
                   ssd_a_log, ssd_d, ssd_norm_w, w_out):
    pad_rows = lambda w: jnp.pad(w, ((0, SUBLANES - w.shape[0]), (0, 0)))
    pad_lanes = lambda v: jnp.pad(v, (0, LANES - v.shape[0]))[None, :]
    win = jnp.pad(w_in, ((0, 0), (0, D_IN_PAD - w_in.shape[1]))).astype(BF16)
    return (norm_mix_w[None, :], win, pad_rows(conv_short_w), conv_norm_w[None, :], pad_rows(ssd_conv_w),
            ssd_conv_b[None, :], pad_lanes(ssd_dt_bias), pad_lanes(ssd_a_log),
            jnp.repeat(ssd_d, SSD_HEAD_DIM)[None, :], ssd_norm_w[None, :], w_out.astype(BF16))


def _run_mixer(x, meta_tokens, wts):
    zeros = lambda *s: jnp.zeros(s, F32)
    meta = jnp.concatenate([zeros(CHUNK - N_META, D_MODEL), meta_tokens.astype(F32)], axis=0)[None]
    _, st0, tu0, tx0 = _mixer(meta, zeros(D_STATE, D_SSD), zeros(SUBLANES, D_MODEL), zeros(SUBLANES, D_XBC),
                              wts, rows=CHUNK, n_inert=CHUNK - N_META)
    h, _, _, _ = _mixer(x, st0[0], tu0[0], tx0[0], wts, rows=MIX_ROWS, n_inert=0)
    return h


def _row_tiles_store(ref, val, rows):
    for j in range(LANES_PER_ROW):
        ref[pl.ds(j, rows, stride=SUBLANES), :] = val[:, j * LANES:(j + 1) * LANES]


def _row_tiles_load(ref, first_row, rows):
    return jnp.concatenate(
        [ref[pl.ds(first_row * SUBLANES + j, rows, stride=SUBLANES), :] for j in range(LANES_PER_ROW)], axis=1)


def _router_kernel(h_ref, nw_ref, rw_ref, rb_ref, xn_ref, idx_ref, rank_ref, gate_ref, cnt_ref, carry, *, rows):
    @pl.when(pl.program_id(0) == 0)
    def _():
        carry[...] = jnp.zeros_like(carry)

    xn = _rms(h_ref[...], nw_ref[...])
    _row_tiles_store(xn_ref, xn, rows)

    lane = lax.broadcasted_iota(I32, (rows, LANES), 1).astype(F32)
    x_hi = xn.astype(BF16)
    x_lo = (xn - x_hi.astype(F32)).astype(BF16)
    w = rw_ref[...]
    w_hi = w.astype(BF16)
    w_lo = (w - w_hi.astype(F32)).astype(BF16)
    logits = _dot(x_hi, w_hi) + _dot(x_lo, w_hi) + _dot(x_hi, w_lo) + rb_ref[...]
    cur = jnp.where(lane < N_EXPERTS, logits, -jnp.inf)
    idxs, vals = [], []
    for _ in range(TOP_K):
        m = jnp.max(cur, axis=1, keepdims=True)
        ik = jnp.min(jnp.where(cur == m, lane, float(LANES)), axis=1, keepdims=True)
        cur = jnp.where(lane == ik, -jnp.inf, cur)
        idxs.append(ik)
        vals.append(m)
    es = [jnp.exp(v - vals[0]) for v in vals]
    den = es[0] + es[1] + es[2] + es[3]
    gates = [e / den for e in es]

    onehot = jnp.zeros((rows, LANES), F32)
    for ik in idxs:
        onehot = onehot + (lane == ik).astype(F32)
    r_i = lax.broadcasted_iota(I32, (rows, rows), 0)
    c_i = lax.broadcasted_iota(I32, (rows, rows), 1)
    before = (r_i > c_i).astype(BF16)
    rankmat = _dot(before, onehot.astype(BF16)) + carry[...]
    carry[...] = carry[...] + jnp.sum(onehot, axis=0, keepdims=True)
    ranks = [jnp.sum(jnp.where(lane == ik, rankmat, 0.0), axis=1, keepdims=True) for ik in idxs]

    def pack(cols):
        m = jnp.zeros((rows, LANES), F32)
        for k, col in enumerate(cols):
            m = jnp.where(lane == k, col, m)
        return m

    idx_ref[...] = pack(idxs).T[0:SUBLANES, :].astype(I32)
    rank_ref[...] = pack(ranks).T[0:SUBLANES, :].astype(I32)
    gate_ref[...] = pack(gates)
    cnt_ref[...] = jnp.broadcast_to(carry[...], (SUBLANES, LANES))


def _router(h2, nw, rw, rb):
    n_tok = h2.shape[0]
    rows = ROUTE_ROWS
    const = lambda shp: pl.BlockSpec(shp, lambda i: (0,) * len(shp))
    return pl.pallas_call(
        functools.partial(_router_kernel, rows=rows),
        grid=(n_tok // rows,),
        in_specs=[pl.BlockSpec((rows, D_MODEL), lambda i: (i, 0)), const((1, D_MODEL)),
                  const((D_MODEL, LANES)), const((1, LANES))],
        out_specs=[
            pl.BlockSpec((rows * SUBLANES, LANES), lambda i: (i, 0)),
            pl.BlockSpec((SUBLANES, rows), lambda i: (0, i)),
            pl.BlockSpec((SUBLANES, rows), lambda i: (0, i)),
            pl.BlockSpec((rows, LANES), lambda i: (i, 0)),
            const((SUBLANES, LANES)),
        ],
        out_shape=[
            jax.ShapeDtypeStruct((n_tok * SUBLANES, LANES), F32),
            jax.ShapeDtypeStruct((SUBLANES, n_tok), I32),
            jax.ShapeDtypeStruct((SUBLANES, n_tok), I32),
            jax.ShapeDtypeStruct((n_tok, LANES), F32),
            jax.ShapeDtypeStruct((SUBLANES, LANES), F32),
        ],
        scratch_shapes=[pltpu.VMEM((1, LANES), F32)],
        compiler_params=pltpu.CompilerParams(dimension_semantics=("arbitrary",), vmem_limit_bytes=VMEM_LIMIT),
        name="router",
    )(h2, nw, rw, rb)


def _dest_kernel(start_ref, idx_ref, rank_ref, dest_ref):
    idx = idx_ref[...]
    dest = rank_ref[...]
    for e in range(N_EXPERTS):
        dest = dest + jnp.where(idx == e, start_ref[e], 0)
    dest_ref[...] = dest


def _dest(pad_start, idx_t, rank_t):
    n_tok = idx_t.shape[1]
    cols = min(n_tok, 4096)
    spec = pl.BlockSpec((SUBLANES, cols), lambda i, s: (0, i))
    return pl.pallas_call(
        _dest_kernel,
        grid_spec=pltpu.PrefetchScalarGridSpec(
            num_scalar_prefetch=1, grid=(n_tok // cols,), in_specs=[spec, spec], out_specs=spec),
        out_shape=jax.ShapeDtypeStruct((SUBLANES, n_tok), I32),
        name="dest",
    )(pad_start, idx_t, rank_t)


def _slot_rows_kernel(zstart_ref, zcount_ref, zoff_ref, dest_ref, rows_ref, *, n_tok):
    i = pl.program_id(0)
    base = i * SLOT_CHUNK

    def body(g, c):
        for u in range(ISSUE_UNROLL):
            a = g * ISSUE_UNROLL + u
            rows_ref[dest_ref[a]] = base + a
        return c

    lax.fori_loop(0, SLOT_CHUNK // ISSUE_UNROLL, body, 0)

    @pl.when(i == pl.num_programs(0) - 1)
    def _():
        def fill_segment(s, c):
            first = zstart_ref[s]
            spare = n_tok * TOP_K + zoff_ref[s]

            def fill(j, c2):
                rows_ref[first + j] = spare + j
                return c2

            lax.fori_loop(0, zcount_ref[s], fill, 0)
            return c

        lax.fori_loop(0, N_EXPERTS + 1, fill_segment, 0)


def _slot_rows(zstart, zcount, zoff, dest_flat, n_rows, n_tok):
    return pl.pallas_call(
        functools.partial(_slot_rows_kernel, n_tok=n_tok),
        grid_spec=pltpu.PrefetchScalarGridSpec(
            num_scalar_prefetch=3, grid=(dest_flat.shape[0] // SLOT_CHUNK,),
            in_specs=[pl.BlockSpec((SLOT_CHUNK,), lambda i, a, b, c: (i,), memory_space=pltpu.SMEM)],
            out_specs=pl.BlockSpec(memory_space=pltpu.SMEM)),
        out_shape=jax.ShapeDtypeStruct((n_rows,), I32),
        compiler_params=pltpu.CompilerParams(dimension_semantics=("arbitrary",)),
        name="slot_rows",
    )(zstart, zcount, zoff, dest_flat)


def _tile_rows(row):
    return pl.ds(pl.multiple_of(row * SUBLANES, SUBLANES), SUBLANES)


def _expert_mlp_half(x_bf, wgu_bf, bgu_ref, wd_bf, j):
    c_glu = slice(j * FF_COLS, (j + 1) * FF_COLS)
    c_lin = slice(D_FF + j * FF_COLS, D_FF + (j + 1) * FF_COLS)
    glu = jnp.minimum(_dot(x_bf, wgu_bf[:, c_glu]) + bgu_ref[:, c_glu], SWIGLU_LIMIT)
    lin = jnp.clip(_dot(x_bf, wgu_bf[:, c_lin]) + bgu_ref[:, c_lin], -SWIGLU_LIMIT, SWIGLU_LIMIT)
    act = glu * (1.0 / (1.0 + jnp.exp(-SWIGLU_ALPHA * glu))) * (lin + 1.0)
    return _dot(act.astype(BF16), wd_bf[c_glu, :])


def _expert_kernel(be_ref, rows_cur, rows_next, rows_prev, xn_hbm, wgu_ref, bgu_ref, wd_ref, bd_ref, ytok_hbm,
                   xbuf, ybuf, wgu_bf, wd_bf, gsem, ssem, *, rows, n_blocks, n_tok):
    b = pl.program_id(0)
    cur = lax.rem(b, 2)
    oth = 1 - cur
    blk = rows * SUBLANES

    def gather_starts(slot_rows, buf):
        def thunk(r):
            def run():
                tok = slot_rows[r] & (n_tok - 1)
                pltpu.make_async_copy(xn_hbm.at[_tile_rows(tok), :], xbuf.at[buf, pl.ds(r * SUBLANES, SUBLANES), :],
                                      gsem.at[buf]).start(priority=r % 2)
            return run
        return [thunk(r) for r in range(rows)]

    def scatter_starts(buf):
        def thunk(r):
            def run():
                pltpu.make_async_copy(ybuf.at[buf, pl.ds(r * SUBLANES, SUBLANES), :],
                                      ytok_hbm.at[_tile_rows(rows_prev[r]), :], ssem.at[buf]).start(priority=r % 2)
            return run
        return [thunk(r) for r in range(rows)]

    def gather_wait(buf):
        pltpu.make_async_copy(xn_hbm.at[pl.ds(0, blk), :], xbuf.at[buf], gsem.at[buf]).wait()

    def scatter_wait(buf):
        pltpu.make_async_copy(ybuf.at[buf], ytok_hbm.at[pl.ds(0, blk), :], ssem.at[buf]).wait()

    def cast_weights_if_new_expert():
        @pl.when((b == 0) | (be_ref[b] != be_ref[jnp.maximum(b - 1, 0)]))
        def _():
            for r in range(0, D_MODEL, CAST_ROWS):
                wgu_bf[r:r + CAST_ROWS, :] = wgu_ref[r:r + CAST_ROWS, :].astype(BF16)
                wd_bf[r:r + CAST_ROWS, :] = wd_ref[r:r + CAST_ROWS, :].astype(BF16)

    def half(j):
        x_bf = _row_tiles_load(xbuf.at[cur], 0, rows).astype(BF16)
        return _expert_mlp_half(x_bf, wgu_bf, bgu_ref, wd_bf, j)

    @pl.when(b == 0)
    def _():
        for start in gather_starts(rows_cur, 0):
            start()
        cast_weights_if_new_expert()
        gather_wait(0)
        for start in gather_starts(rows_next, 1):
            start()
        _row_tiles_store(ybuf.at[0], bd_ref[...] + half(0) + half(1), rows)

    @pl.when((b > 0) & (b < n_blocks))
    def _():
        for start in gather_starts(rows_next, oth):
            start()
        cast_weights_if_new_expert()
        gather_wait(cur)
        y = bd_ref[...] + half(0)
        for start in scatter_starts(oth):
            start()

        @pl.when(b >= 2)
        def _():
            scatter_wait(cur)

        _row_tiles_store(ybuf.at[cur], y + half(1), rows)

    @pl.when(b == n_blocks)
    def _():
        for start in scatter_starts(oth):
            start()
        scatter_wait(oth)
        scatter_wait(cur)
        gather_wait(cur)


def _experts(blk_expert, slot_rows, xn2, wgu, bgu, wd, bd):
    rows = MOE_BLOCK
    n_rows = slot_rows.shape[0]
    n_blocks = n_rows // rows
    n_tok = xn2.shape[0] // SUBLANES
    last = n_blocks - 1
    by_expert = lambda shp: pl.BlockSpec((None,) + shp, lambda b, be: (be[jnp.minimum(b, last)], 0, 0))
    slot_block = lambda shift: pl.BlockSpec((rows,), lambda b, be: (jnp.clip(b + shift, 0, last),),
                                            memory_space=pltpu.SMEM)
    return pl.pallas_call(
        functools.partial(_expert_kernel, rows=rows, n_blocks=n_blocks, n_tok=n_tok),
        grid_spec=pltpu.PrefetchScalarGridSpec(
            num_scalar_prefetch=1, grid=(n_blocks + 1,),
            in_specs=[slot_block(0), slot_block(1), slot_block(-1), pl.BlockSpec(memory_space=pl.ANY),
                      by_expert((D_MODEL, 2 * D_FF)), by_expert((1, 2 * D_FF)),
                      by_expert((D_FF, D_MODEL)), by_expert((1, D_MODEL))],
            out_specs=pl.BlockSpec(memory_space=pl.ANY),
            scratch_shapes=[pltpu.VMEM((2, rows * SUBLANES, LANES), F32), pltpu.VMEM((2, rows * SUBLANES, LANES), F32),
                            pltpu.VMEM((D_MODEL, 2 * D_FF), BF16), pltpu.VMEM((D_FF, D_MODEL), BF16),
                            pltpu.SemaphoreType.DMA((2,)), pltpu.SemaphoreType.DMA((2,))]),
        out_shape=jax.ShapeDtypeStruct((n_rows * SUBLANES, LANES), F32),
        compiler_params=pltpu.CompilerParams(dimension_semantics=("arbitrary",), vmem_limit_bytes=VMEM_LIMIT,
                                             disable_bounds_checks=True),
        name="experts",
    )(blk_expert, slot_rows, slot_rows, slot_rows, xn2, wgu, bgu, wd, bd)


def _combine_kernel(h_ref, gate_ref, fw_ref, y0_ref, y1_ref, y2_ref, y3_ref, out_ref, *, tokens):
    acc = h_ref[...]
    gate = gate_ref[...]
    for k, y_ref in enumerate((y0_ref, y1_ref, y2_ref, y3_ref)):
        acc = acc + gate[:, k:k + 1] * _row_tiles_load(y_ref, 0, tokens)
    out_ref[...] = _rms(acc, fw_ref[...])


def _combine(h2, gates, fw, ytok):
    n_tok = h2.shape[0]
    tokens = COMBINE_ROWS
    n_tiles = n_tok // tokens
    y_spec = lambda k: pl.BlockSpec((tokens * SUBLANES, LANES), lambda i: (k * n_tiles + i, 0))
    return pl.pallas_call(
        functools.partial(_combine_kernel, tokens=tokens),
        grid=(n_tiles,),
        in_specs=[pl.BlockSpec((tokens, D_MODEL), lambda i: (i, 0)),
                  pl.BlockSpec((tokens, LANES), lambda i: (i, 0)),
                  pl.BlockSpec((1, D_MODEL), lambda i: (0, 0))] + [y_spec(k) for k in range(TOP_K)],
        out_specs=pl.BlockSpec((tokens, D_MODEL), lambda i: (i, 0)),
        out_shape=jax.ShapeDtypeStruct((n_tok, D_MODEL), F32),
        compiler_params=pltpu.CompilerParams(dimension_semantics=("arbitrary",), vmem_limit_bytes=VMEM_LIMIT),
        name="combine",
    )(h2, gates, fw, ytok, ytok, ytok, ytok)


def _moe(h2, norm_ffn_w, router_w, router_b, w_gate_up, b_gate_up, w_down, b_down, norm_final_w):
    n_tok = h2.shape[0]
    n_assign = n_tok * TOP_K
    n_blocks = -(-(n_assign + N_EXPERTS * (MOE_BLOCK - 1)) // MOE_BLOCK)
    n_rows = n_blocks * MOE_BLOCK

    rw = jnp.pad(router_w, ((0, 0), (0, LANES - N_EXPERTS)))
    rb = jnp.pad(router_b, (0, LANES - N_EXPERTS))[None, :]
    xn2, idx_t, rank_t, gates, cnt = _router(h2, norm_ffn_w[None, :], rw, rb)

    counts = cnt[0, :N_EXPERTS].astype(I32)
    padded = ((counts + MOE_BLOCK - 1) // MOE_BLOCK) * MOE_BLOCK
    pad_end = jnp.cumsum(padded)
    pad_start = pad_end - padded
    blk_first_row = jnp.arange(n_blocks, dtype=I32) * MOE_BLOCK
    blk_expert = jnp.minimum(jnp.sum((pad_end[None, :] <= blk_first_row[:, None]).astype(I32), axis=1),
                             N_EXPERTS - 1)
    zstart = jnp.concatenate([pad_start + counts, pad_end[-1:]]).astype(I32)
    zcount = jnp.concatenate([padded - counts, n_rows - pad_end[-1:]]).astype(I32)
    zoff = (jnp.cumsum(zcount) - zcount).astype(I32)

    dest_t = _dest(pad_start.astype(I32), idx_t, rank_t)
    dest_flat = dest_t[:TOP_K].reshape(-1)
    slot_rows = _slot_rows(zstart, zcount, zoff, dest_flat, n_rows, n_tok)

    ytok = _experts(blk_expert, slot_rows, xn2, w_gate_up, b_gate_up[:, None, :], w_down, b_down[:, None, :])
    return _combine(h2, gates, norm_final_w[None, :], ytok)


def kernel(x, meta_tokens, norm_mix_w, w_in, conv_short_w, conv_norm_w, ssd_conv_w, ssd_conv_b, ssd_dt_bias,
           ssd_a_log, ssd_d, ssd_norm_w, w_out, norm_ffn_w, router_w, router_b, w_gate_up, b_gate_up, w_down,
           b_down, norm_final_w):
    bsz, seq, _ = x.shape
    wts = _mixer_weights(norm_mix_w[0], w_in[0], conv_short_w[0], conv_norm_w[0], ssd_conv_w[0], ssd_conv_b[0],
                         ssd_dt_bias[0], ssd_a_log[0], ssd_d[0], ssd_norm_w[0], w_out[0])
    h = _run_mixer(x, meta_tokens, wts)
    out = _moe(h.reshape(bsz * seq, D_MODEL), norm_ffn_w[0], router_w[0], router_b[0], w_gate_up[0],
               b_gate_up[0], w_down[0], b_down[0], norm_final_w)
    return out.reshape(bsz, seq, D_MODEL)
```
